```python
import jax, jax.numpy as jnp
from jax import lax
import numpy as np

D_MODEL = 1024
BATCH = 8
SEQ = 2048
DEPTH = 1
DEC_BATCH = 128
DEC_SEQ = 4
PAST_LEN = 16384
PAGE_SIZE = 128

N_META = 16
CONV_CH = D_MODEL // 2
SCONV_K = 3
GDN_HEADS = 4
GDN_DK = 128
GDN_DV = 128
GDN_CONV_K = 4
CHUNK = 64
FFN_HIDDEN = 4 * D_MODEL
EPS = 1e-6

QKV_W = GDN_HEADS * (2 * GDN_DK + GDN_DV)
O_B = 0
O_C = O_B + CONV_CH
O_H = O_C + CONV_CH
O_QKV = O_H + CONV_CH
O_G = O_QKV + QKV_W
O_A = O_G + GDN_HEADS * GDN_DV
O_BETA = O_A + GDN_HEADS
IN_W = O_BETA + GDN_HEADS
MIX_W = CONV_CH + GDN_HEADS * GDN_DV

kernel_name = 'hymba_shortconv_gated_deltanet_step'


def rmsnorm(x, g):
    xf = x.astype(jnp.float32)
    y = xf * lax.rsqrt(jnp.mean(xf * xf, axis=-1, keepdims=True) + EPS)
    return (y * g.astype(jnp.float32)).astype(x.dtype)


def l2norm(x):
    return x * lax.rsqrt(jnp.sum(x * x, axis=-1, keepdims=True) + EPS)


def causal_dwconv(x, buf, w):
    xp = jnp.concatenate([buf.astype(x.dtype), x], axis=1)
    y = lax.conv_general_dilated(xp, w[:, None, :].astype(x.dtype), window_strides=(1,), padding='VALID',
                                 dimension_numbers=('NWC', 'WIO', 'NWC'), feature_group_count=x.shape[-1])
    return y, xp[:, xp.shape[1] - (w.shape[0] - 1):]


def chunk_len(t):
    return CHUNK if t % CHUNK == 0 else t


def gated_delta_chunked(q, k, v, g, beta, state, chunk):
    b, t, h, _ = q.shape
    dv = v.shape[-1]
    n = t // chunk

    def blocks(a):
        a = a.reshape((b, n, chunk, h) + a.shape[3:])
        return jnp.moveaxis(jnp.moveaxis(a, 1, 0), 2, 3)

    qc, kc, vc, gc, bc = blocks(q), blocks(k), blocks(v), blocks(g), blocks(beta)
    G = jnp.cumsum(gc, axis=-1)
    idx = jnp.arange(chunk)
    causal = idx[:, None] >= idx[None, :]
    strict = idx[:, None] > idx[None, :]
    decay = jnp.exp(jnp.where(causal, G[..., :, None] - G[..., None, :], -jnp.inf))
    kb = kc * bc[..., None]
    lower = jnp.where(strict, jnp.einsum('nbhid,nbhjd->nbhij', kb, kc) * decay, 0.0)
    eye = jnp.eye(chunk, dtype=jnp.float32)
    rhs = jnp.concatenate([vc * bc[..., None], kb * jnp.exp(G)[..., None]], axis=-1)
    sol = lax.linalg.triangular_solve(eye + lower, rhs, left_side=True, lower=True)
    u, w = sol[..., :dv], sol[..., dv:]
    qk = jnp.einsum('nbhid,nbhjd->nbhij', qc, kc) * decay
    q_dec = qc * jnp.exp(G)[..., None]
    g_last = G[..., -1]
    k_dec = kc * jnp.exp(g_last[..., None] - G)[..., None]

    def step(S, xs):
        u_i, w_i, qk_i, qd_i, kd_i, gl_i = xs
        v_new = u_i - jnp.einsum('bhcd,bhde->bhce', w_i, S)
        o_i = jnp.einsum('bhcd,bhde->bhce', qd_i, S) + jnp.einsum('bhij,bhje->bhie', qk_i, v_new)
        S = S * jnp.exp(gl_i)[..., None, None] + jnp.einsum('bhcd,bhce->bhde', kd_i, v_new)
        return S, o_i

    S, o = lax.scan(step, state, (u, w, qk, q_dec, k_dec, g_last))
    o = jnp.moveaxis(jnp.moveaxis(o, 3, 2), 0, 1).reshape(b, t, h, dv)
    return o, S


def hybrid_layer(h, buf_a, buf_qkv, ssm, segments, norm_mix, w_in, conv_a_w, conv_qkv_w, a_log, dt_bias,
                 gdn_norm, w_out, norm_ffn, w_up, w_down):
    f32 = jnp.float32
    bsz, t, _ = h.shape
    xn = rmsnorm(h, norm_mix)
    z = xn @ w_in
    conv_out, new_a = causal_dwconv(z[..., O_C:O_H] * z[..., O_H:O_QKV], buf_a, conv_a_w)
    y_a = z[..., O_B:O_C] * conv_out
    qkv, new_qkv = causal_dwconv(z[..., O_QKV:O_G], buf_qkv, conv_qkv_w)
    qkv = jax.nn.silu(qkv.astype(f32))
    nq = GDN_HEADS * GDN_DK
    q = l2norm(qkv[..., :nq].reshape(bsz, t, GDN_HEADS, GDN_DK)) * (GDN_DK ** -0.5)
    k = l2norm(qkv[..., nq:2 * nq].reshape(bsz, t, GDN_HEADS, GDN_DK))
    v = qkv[..., 2 * nq:].reshape(bsz, t, GDN_HEADS, GDN_DV)
    g = -jnp.exp(a_log.astype(f32)) * jax.nn.softplus(z[..., O_A:O_BETA].astype(f32) + dt_bias.astype(f32))
    beta = jax.nn.sigmoid(z[..., O_BETA:IN_W].astype(f32))
    S = ssm.astype(f32)
    outs = []
    start = 0
    for seg in segments:
        sl = slice(start, start + seg)
        o_s, S = gated_delta_chunked(q[:, sl], k[:, sl], v[:, sl], g[:, sl], beta[:, sl], S, chunk_len(seg))
        outs.append(o_s)
        start += seg
    o = jnp.concatenate(outs, axis=1)
    gate = z[..., O_G:O_A].reshape(bsz, t, GDN_HEADS, GDN_DV).astype(f32)
    o = rmsnorm(o, gdn_norm) * jax.nn.silu(gate)
    y_b = o.reshape(bsz, t, GDN_HEADS * GDN_DV).astype(h.dtype)
    h = h + jnp.concatenate([y_a, y_b], axis=-1) @ w_out
    u = jax.nn.relu(rmsnorm(h, norm_ffn) @ w_up)
    h = h + (u * u) @ w_down
    return h, new_a, new_qkv, S.astype(ssm.dtype)


def setup_inputs(seed: int = 0) -> dict:
    key = jax.random.key(seed)
    ks = jax.random.split(key, 20)
    nrm = jax.random.normal
    f32 = jnp.float32
    x_prompt = nrm(ks[0], (BATCH, SEQ, D_MODEL), f32)
    x_sample = nrm(ks[1], (DEC_BATCH, DEC_SEQ, D_MODEL), f32)
    state_conv_a = nrm(ks[2], (DEPTH, DEC_BATCH, SCONV_K - 1, CONV_CH), f32)
    state_conv_qkv = nrm(ks[3], (DEPTH, DEC_BATCH, GDN_CONV_K - 1, QKV_W), f32)
    state_ssm = 0.3 * nrm(ks[4], (DEPTH, DEC_BATCH, GDN_HEADS, GDN_DK, GDN_DV), f32)
    meta_tokens = nrm(ks[5], (N_META, D_MODEL), f32)
    norm_mix = 1.0 + 0.02 * nrm(ks[6], (DEPTH, D_MODEL), f32)
    w_in = nrm(ks[7], (DEPTH, D_MODEL, IN_W), f32) * D_MODEL ** -0.5
    conv_a_w = nrm(ks[8], (DEPTH, SCONV_K, CONV_CH), f32) * SCONV_K ** -0.5
    conv_qkv_w = nrm(ks[9], (DEPTH, GDN_CONV_K, QKV_W), f32) * GDN_CONV_K ** -0.5
    a_log = jnp.log(jax.random.uniform(ks[10], (DEPTH, GDN_HEADS), f32, 1.0, 16.0))
    dt = jnp.exp(jax.random.uniform(ks[11], (DEPTH, GDN_HEADS), f32, np.log(1e-3), np.log(1e-1)))
    dt_bias = dt + jnp.log(-jnp.expm1(-dt))
    gdn_norm = 1.0 + 0.02 * nrm(ks[12], (DEPTH, GDN_DV), f32)
    w_out = nrm(ks[13], (DEPTH, MIX_W, D_MODEL), f32) * MIX_W ** -0.5
    norm_ffn = 1.0 + 0.02 * nrm(ks[14], (DEPTH, D_MODEL), f32)
    w_up = nrm(ks[15], (DEPTH, D_MODEL, FFN_HIDDEN), f32) * D_MODEL ** -0.5
    w_down = nrm(ks[16], (DEPTH, FFN_HIDDEN, D_MODEL), f32) * FFN_HIDDEN ** -0.5
    norm_final = 1.0 + 0.02 * nrm(ks[17], (D_MODEL,), f32)
    return {'x_prompt': x_prompt, 'x_sample': x_sample, 'state_conv_a': state_conv_a,
            'state_conv_qkv': state_conv_qkv, 'state_ssm': state_ssm, 'meta_tokens': meta_tokens,
            'norm_mix': norm_mix, 'w_in': w_in, 'conv_a_w': conv_a_w, 'conv_qkv_w': conv_qkv_w,
            'a_log': a_log, 'dt_bias': dt_bias, 'gdn_norm': gdn_norm, 'w_out': w_out,
            'norm_ffn': norm_ffn, 'w_up': w_up, 'w_down': w_down, 'norm_final': norm_final}


def reference(x_prompt, x_sample, state_conv_a, state_conv_qkv, state_ssm, meta_tokens, norm_mix, w_in,
              conv_a_w, conv_qkv_w, a_log, dt_bias, gdn_norm, w_out, norm_ffn, w_up, w_down, norm_final):
    bp, tp, _ = x_prompt.shape
    bs, ts, _ = x_sample.shape
    meta = jnp.broadcast_to(meta_tokens[None].astype(x_prompt.dtype), (bp, N_META, D_MODEL))
    hp = jnp.concatenate([meta, x_prompt], axis=1)
    hs = x_sample
    pa, pq, ps, sa, sq, ss = [], [], [], [], [], []
    for l in range(DEPTH):
        lw = (norm_mix[l], w_in[l], conv_a_w[l], conv_qkv_w[l], a_log[l], dt_bias[l], gdn_norm[l],
              w_out[l], norm_ffn[l], w_up[l], w_down[l])
        hp, a_p, q_p, s_p = hybrid_layer(
            hp, jnp.zeros((bp, SCONV_K - 1, CONV_CH), x_prompt.dtype),
            jnp.zeros((bp, GDN_CONV_K - 1, QKV_W), x_prompt.dtype),
            jnp.zeros((bp, GDN_HEADS, GDN_DK, GDN_DV), state_ssm.dtype), (N_META, tp), *lw)
        hs, a_s, q_s, s_s = hybrid_layer(hs, state_conv_a[l], state_conv_qkv[l], state_ssm[l], (ts,), *lw)
        pa.append(a_p); pq.append(q_p); ps.append(s_p)
        sa.append(a_s.astype(state_conv_a.dtype)); sq.append(q_s.astype(state_conv_qkv.dtype)); ss.append(s_s)
    y_prompt = rmsnorm(hp, norm_final)[:, N_META:]
    y_sample = rmsnorm(hs, norm_final)
    return (y_prompt, y_sample, jnp.stack(pa), jnp.stack(pq), jnp.stack(ps), jnp.stack(sa), jnp.stack(sq), jnp.stack(ss))
```

```python
import dataclasses
import functools

import jax
import jax.numpy as jnp
from jax import lax
from jax.experimental import pallas as pl
from jax.experimental.pallas import tpu as pltpu

F32 = jnp.float32
BF16 = jnp.bfloat16

D_MODEL = 1024
CONV_CH = 512
SCONV_K = 3
GDN_HEADS = 4
GDN_DK = 128
GDN_DV = 128
GDN_CONV_K = 4
QKV_W = GDN_HEADS * (2 * GDN_DK + GDN_DV)
FFN_HIDDEN = 4 * D_MODEL
EPS = 1e-6

O_B = 0
O_C = O_B + CONV_CH
O_H = O_C + CONV_CH
O_QKV = O_H + CONV_CH
O_G = O_QKV + QKV_W
O_AB = O_G + GDN_HEADS * GDN_DV
LANES = 128
SUBLANES = 8
IN_W_PAD = O_AB + LANES
VC_ROWS = 128
TAIL_PAD = SUBLANES
NEUMANN_BASE = 16
VMEM_LIMIT_BYTES = 56 * 1024 * 1024


@dataclasses.dataclass(frozen=True)
class Cfg:
    batch: int
    nb: int
    seq_len: int
    valid_len: int
    chunk: int
    n_tt: int
    shared_state: bool

    @property
    def rows(self):
        return self.nb * self.seq_len

    @property
    def carry(self):
        return self.nb == 1


def _bf(x):
    return x.astype(BF16)


def _dot(a, b):
    return jnp.dot(a, b, preferred_element_type=F32)


def _dot1(a, b):
    return _dot(_bf(a), _bf(b))


def _dot1_nt(a, b):
    return lax.dot_general(_bf(a), _bf(b), (((1,), (1,)), ((), ())), preferred_element_type=F32)


def _split2(x):
    hi = _bf(x)
    lo = _bf(x - hi.astype(F32))
    return hi, lo


def _dot3(a, b):
    ah, al = _split2(a)
    bh, bl = _split2(b)
    return _dot(ah, bh) + (_dot(ah, bl) + _dot(al, bh))


def _rms(x, g_row):
    return x * lax.rsqrt(jnp.mean(x * x, axis=-1, keepdims=True) + EPS) * g_row


def _softplus(x):
    return jnp.maximum(x, 0.0) + jnp.log1p(jnp.exp(-jnp.abs(x)))


def _silu(x):
    return x * jax.nn.sigmoid(x)


def _log2(n):
    k = n.bit_length() - 1
    assert (1 << k) == n, n
    return k


def _unit_lower_inverse(a, chunk, ii, jj):
    base = min(chunk, NEUMANN_BASE)
    lb = _log2(base)
    eye = (ii == jj).astype(F32)
    p = jnp.where((ii >> lb) == (jj >> lb), a, 0.0)
    t = eye - p
    for _ in range(lb - 1):
        p = _dot3(p, p)
        t = t + _dot3(t, p)
    size = base
    while size < chunk:
        ls = _log2(size)
        off = jnp.where(((ii >> (ls + 1)) == (jj >> (ls + 1))) & ((ii >> ls) != (jj >> ls)), a, 0.0)
        t = t - _dot3(_dot3(t, off), t)
        size *= 2
    return t


def _vc_blocks(cfg, unit):
    c = cfg.chunk
    if cfg.carry:
        r0 = unit * c
        per = VC_ROWS // c
        return [[(r0, h, 0) for h in range(v * per, (v + 1) * per)] for v in range(GDN_HEADS // per)]
    nseq = VC_ROWS // (c * GDN_HEADS)
    return [[((unit * nseq + s) * cfg.seq_len, h, unit * nseq + s) for h in range(GDN_HEADS) for s in range(nseq)]]


def _delta_vc(cfg, blocks, qkv_scr, gb_scr, gc_scr, o_scr, s_get, s_put):
    c = cfg.chunk
    lc = _log2(c)

    def gather(ref, col, width):
        return jnp.concatenate([ref[r0:r0 + c, col(h):col(h) + width] for (r0, h, _) in blocks], axis=0)

    q = gather(qkv_scr, lambda h: GDN_DK * h, GDN_DK)
    k = gather(qkv_scr, lambda h: GDN_HEADS * GDN_DK + GDN_DK * h, GDN_DK)
    v = gather(qkv_scr, lambda h: 2 * GDN_HEADS * GDN_DK + GDN_DV * h, GDN_DV)
    beta = gather(gb_scr, lambda h: GDN_HEADS + h, 1)
    gcum = gather(gc_scr, lambda h: h, 1)

    ii = lax.broadcasted_iota(jnp.int32, (VC_ROWS, VC_ROWS), 0)
    jj = lax.broadcasted_iota(jnp.int32, (VC_ROWS, VC_ROWS), 1)
    same = (ii >> lc) == (jj >> lc)
    g_rows = jnp.broadcast_to(gcum, (VC_ROWS, VC_ROWS)).T
    decay = jnp.where(same & (jj <= ii), jnp.exp(gcum - g_rows), 0.0)
    g_last = jnp.sum(jnp.where(jj == (ii | (c - 1)), g_rows, 0.0), axis=1, keepdims=True)

    kk = _dot1_nt(k, k)
    qk = _dot1_nt(q, k) * decay
    a = jnp.where(jj < ii, kk * beta * decay, 0.0)
    t = _unit_lower_inverse(a, c, ii, jj)
    exp_g = jnp.exp(gcum)
    rhs = jnp.concatenate([v * beta, k * (beta * exp_g)], axis=1)
    uw = _dot3(t, rhs)
    u = uw[:, :GDN_DV]
    w = uw[:, GDN_DV:]
    q_dec = q * exp_g
    k_dec = k * jnp.exp(g_last - gcum)

    states = [s_get(h, s) for (_, h, s) in blocks]
    v_new, q_s = [], []
    for b, st in enumerate(states):
        p0 = b * c
        wq = jnp.concatenate([w[p0:p0 + c], q_dec[p0:p0 + c]], axis=0)
        r = _dot1(wq, st)
        v_new.append(u[p0:p0 + c] - r[:c])
        q_s.append(r[c:])
    v_new = jnp.concatenate(v_new, axis=0)
    o = jnp.concatenate(q_s, axis=0) + _dot1(qk, v_new)
    k_dec_t = _bf(k_dec.T)
    rows = lax.broadcasted_iota(jnp.int32, (VC_ROWS, GDN_DV), 0)
    for b, ((r0, h, s), st) in enumerate(zip(blocks, states)):
        p0 = b * c
        v_blk = jnp.where((rows >> lc) == b, v_new, 0.0)
        s_put(h, s, st * jnp.exp(g_last[p0:p0 + 1, :]) + _dot(k_dec_t, _bf(v_blk)))
        o_scr[r0:r0 + c, GDN_DV * h:GDN_DV * (h + 1)] = o[p0:p0 + c]


def _layer_kernel(cfg, x_ref, ta_ref, tq_ref, ssm_ref, nmix_ref, win_ref, caw_ref, cqw_ref, alog_ref, dt_ref,
                  gnorm_ref, wout_ref, nffn_ref, wup_ref, wdown_ref, nfin_ref,
                  y_ref, nta_ref, ntq_ref, nssm_ref,
                  z_scr, xa_buf, xq_buf, qkv_scr, gb_scr, gc_scr, o_scr, s_scr):
    j = pl.program_id(1)
    nb, sl, rows = cfg.nb, cfg.seq_len, cfg.rows
    first = j == 0
    last = j == cfg.n_tt - 1

    @pl.when(first)
    def _():
        xa_buf[:, TAIL_PAD - (SCONV_K - 1):TAIL_PAD, :] = ta_ref[...]
        xq_buf[:, TAIL_PAD - (GDN_CONV_K - 1):TAIL_PAD, :] = tq_ref[...]
        if cfg.carry:
            s_scr[...] = ssm_ref[0]

    x = x_ref[...].reshape(rows, D_MODEL)
    xn = _rms(x, nmix_ref[...])
    z_scr[...] = _dot(_bf(xn), win_ref[...])

    xa_buf[:, TAIL_PAD:, :] = (z_scr[:, O_C:O_H] * z_scr[:, O_H:O_QKV]).reshape(nb, sl, CONV_CH)
    conv_a = None
    for tap in range(SCONV_K):
        lo = TAIL_PAD - (SCONV_K - 1) + tap
        term = xa_buf[:, lo:lo + sl, :] * caw_ref[tap:tap + 1, :]
        conv_a = term if conv_a is None else conv_a + term
    y_a = z_scr[:, O_B:O_C] * conv_a.reshape(rows, CONV_CH)

    xq_buf[:, TAIL_PAD:, :] = z_scr[:, O_QKV:O_G].reshape(nb, sl, QKV_W)
    conv_q = None
    for tap in range(GDN_CONV_K):
        lo = TAIL_PAD - (GDN_CONV_K - 1) + tap
        term = xq_buf[:, lo:lo + sl, :] * cqw_ref[tap:tap + 1, :]
        conv_q = term if conv_q is None else conv_q + term
    qkv = _silu(conv_q.reshape(rows, QKV_W))
    nq = GDN_HEADS * GDN_DK
    for h in range(GDN_HEADS):
        qh = qkv[:, GDN_DK * h:GDN_DK * (h + 1)]
        kh = qkv[:, nq + GDN_DK * h:nq + GDN_DK * (h + 1)]
        qkv_scr[:, GDN_DK * h:GDN_DK * (h + 1)] = (
            qh * lax.rsqrt(jnp.sum(qh * qh, axis=-1, keepdims=True) + EPS) * (GDN_DK ** -0.5))
        qkv_scr[:, nq + GDN_DK * h:nq + GDN_DK * (h + 1)] = (
            kh * lax.rsqrt(jnp.sum(kh * kh, axis=-1, keepdims=True) + EPS))
    qkv_scr[:, 2 * nq:] = qkv[:, 2 * nq:]

    @pl.when(last)
    def _():
        nta_ref[...] = xa_buf[:, TAIL_PAD + cfg.valid_len - (SCONV_K - 1):TAIL_PAD + cfg.valid_len, :]
        ntq_ref[...] = xq_buf[:, TAIL_PAD + cfg.valid_len - (GDN_CONV_K - 1):TAIL_PAD + cfg.valid_len, :]

    if cfg.n_tt > 1:
        xa_buf[:, TAIL_PAD - (SCONV_K - 1):TAIL_PAD, :] = xa_buf[:, TAIL_PAD + sl - (SCONV_K - 1):TAIL_PAD + sl, :]
        xq_buf[:, TAIL_PAD - (GDN_CONV_K - 1):TAIL_PAD, :] = xq_buf[:, TAIL_PAD + sl - (GDN_CONV_K - 1):TAIL_PAD + sl, :]

    zab = z_scr[:, O_AB:IN_W_PAD]
    g = -jnp.exp(alog_ref[...]) * _softplus(zab + dt_ref[...])
    beta = jax.nn.sigmoid(zab)
    lane = lax.broadcasted_iota(jnp.int32, (rows, LANES), 1)
    gb = jnp.where(lane < GDN_HEADS, g, beta)
    if cfg.valid_len < sl:
        step = lax.broadcasted_iota(jnp.int32, (rows, LANES), 0) & (sl - 1)
        gb = jnp.where(step < cfg.valid_len, gb, 0.0)
    gb_scr[...] = gb
    ri = lax.broadcasted_iota(jnp.int32, (rows, rows), 0)
    rj = lax.broadcasted_iota(jnp.int32, (rows, rows), 1)
    lc = _log2(cfg.chunk)
    tri = jnp.where(((ri >> lc) == (rj >> lc)) & (rj <= ri), 1.0, 0.0).astype(BF16)
    g1 = _bf(gb)
    r1 = gb - g1.astype(F32)
    g2 = _bf(r1)
    g3 = _bf(r1 - g2.astype(F32))
    gc_scr[...] = _dot(tri, g1) + (_dot(tri, g2) + _dot(tri, g3))

    if cfg.carry:
        def s_get(h, s):
            return s_scr[h]

        def s_put(h, s, val):
            s_scr[h] = val
        n_units = -(-cfg.valid_len // cfg.chunk)
        if n_units * cfg.chunk < sl:
            o_scr[n_units * cfg.chunk:, :] = jnp.zeros((sl - n_units * cfg.chunk, GDN_HEADS * GDN_DV), F32)
    else:
        def s_get(h, s):
            return ssm_ref[s, h]

        def s_put(h, s, val):
            nssm_ref[s, h] = val
        n_units = rows * GDN_HEADS // VC_ROWS
    for unit in range(n_units):
        for blocks in _vc_blocks(cfg, unit):
            _delta_vc(cfg, blocks, qkv_scr, gb_scr, gc_scr, o_scr, s_get, s_put)

    if cfg.carry:
        @pl.when(last)
        def _():
            nssm_ref[0] = s_scr[...]

    parts = [y_a]
    for h in range(GDN_HEADS):
        oh = o_scr[:, GDN_DV * h:GDN_DV * (h + 1)]
        gate = z_scr[:, O_G + GDN_DV * h:O_G + GDN_DV * (h + 1)]
        parts.append(_rms(oh, gnorm_ref[...]) * _silu(gate))
    mix = jnp.concatenate(parts, axis=1)
    h1 = x + _dot(_bf(mix), wout_ref[...])

    up = jnp.maximum(_dot(_bf(_rms(h1, nffn_ref[...])), wup_ref[...]), 0.0)
    h2 = h1 + _dot(_bf(up * up), wdown_ref[...])
    y_ref[...] = _rms(h2, nfin_ref[...]).reshape(nb, sl, D_MODEL)


def _layer_call(cfg, name, x, tail_a, tail_q, ssm, params):
    nb, sl, rows = cfg.nb, cfg.seq_len, cfg.rows
    assert x.shape == (cfg.batch, sl * cfg.n_tt, D_MODEL), (x.shape, cfg)
    assert cfg.batch % nb == 0 and sl % SUBLANES == 0 and rows % VC_ROWS == 0
    assert not cfg.shared_state or nb == 1
    n_bblk = cfg.batch // nb

    if cfg.shared_state:
        def state_map(i, j):
            return (0,)
    else:
        def state_map(i, j):
            return (i,)

    def state_spec(shape):
        zeros = (0,) * len(shape)
        return pl.BlockSpec((nb,) + shape, lambda i, j: state_map(i, j) + zeros)

    def out_state_spec(shape):
        zeros = (0,) * len(shape)
        return pl.BlockSpec((nb,) + shape, lambda i, j: (i,) + zeros)

    def const_spec(arr):
        zeros = (0,) * arr.ndim
        return pl.BlockSpec(arr.shape, lambda i, j: zeros, pipeline_mode=pl.Buffered(1))

    ta_shape, tq_shape = (SCONV_K - 1, CONV_CH), (GDN_CONV_K - 1, QKV_W)
    ssm_shape = (GDN_HEADS, GDN_DK, GDN_DV)
    x_spec = pl.BlockSpec((nb, sl, D_MODEL), lambda i, j: (i, j, 0))
    in_specs = [x_spec, state_spec(ta_shape), state_spec(tq_shape), state_spec(ssm_shape)]
    in_specs += [const_spec(p) for p in params]
    out_specs = [x_spec, out_state_spec(ta_shape), out_state_spec(tq_shape), out_state_spec(ssm_shape)]
    out_shape = [
        jax.ShapeDtypeStruct(x.shape, F32),
        jax.ShapeDtypeStruct((cfg.batch,) + ta_shape, F32),
        jax.ShapeDtypeStruct((cfg.batch,) + tq_shape, F32),
        jax.ShapeDtypeStruct((cfg.batch,) + ssm_shape, F32),
    ]
    scratch = [
        pltpu.VMEM((rows, IN_W_PAD), F32),
        pltpu.VMEM((nb, TAIL_PAD + sl, CONV_CH), F32),
        pltpu.VMEM((nb, TAIL_PAD + sl, QKV_W), F32),
        pltpu.VMEM((rows, QKV_W), F32),
        pltpu.VMEM((rows, LANES), F32),
        pltpu.VMEM((rows, LANES), F32),
        pltpu.VMEM((rows, GDN_HEADS * GDN_DV), F32),
        pltpu.VMEM(ssm_shape, F32),
    ]
    return pl.pallas_call(
        functools.partial(_layer_kernel, cfg),
        grid=(n_bblk, cfg.n_tt),
        in_specs=in_specs,
        out_specs=out_specs,
        out_shape=out_shape,
        scratch_shapes=scratch,
        compiler_params=pltpu.CompilerParams(
            dimension_semantics=("arbitrary", "arbitrary"), vmem_limit_bytes=VMEM_LIMIT_BYTES),
        name=name,
    )(x, tail_a, tail_q, ssm, *params)


META_PAD_LEN = 128
PROMPT_TILE = 256
SAMPLE_PAD_LEN = 8
SAMPLE_SEQS_PER_STEP = 16


def kernel(x_prompt, x_sample, state_conv_a, state_conv_qkv, state_ssm, meta_tokens, norm_mix, w_in, conv_a_w,
           conv_qkv_w, a_log, dt_bias, gdn_norm, w_out, norm_ffn, w_up, w_down, norm_final):
    assert norm_mix.shape[0] == 1, "single layer"
    bp, tp, _ = x_prompt.shape
    bs, ts, _ = x_sample.shape
    n_meta = meta_tokens.shape[0]

    def row(v):
        return v.reshape(1, -1).astype(F32)

    def lane_row(v):
        return jnp.pad(v.astype(F32), (0, LANES - v.shape[0])).reshape(1, LANES)

    params = (
        row(norm_mix[0]),
        _bf(jnp.pad(w_in[0], ((0, 0), (0, IN_W_PAD - w_in.shape[2])))),
        conv_a_w[0].astype(F32), conv_qkv_w[0].astype(F32),
        lane_row(a_log[0]), lane_row(dt_bias[0]), row(gdn_norm[0]),
        _bf(w_out[0]), row(norm_ffn[0]), _bf(w_up[0]), _bf(w_down[0]), row(norm_final),
    )

    meta_cfg = Cfg(batch=1, nb=1, seq_len=META_PAD_LEN, valid_len=n_meta, chunk=64, n_tt=1, shared_state=False)
    meta_x = jnp.pad(meta_tokens.astype(F32), ((0, META_PAD_LEN - n_meta), (0, 0)))[None]
    _, ta_m, tq_m, ssm_m = _layer_call(
        meta_cfg, "layer_meta", meta_x,
        jnp.zeros((1, SCONV_K - 1, CONV_CH), F32), jnp.zeros((1, GDN_CONV_K - 1, QKV_W), F32),
        jnp.zeros((1, GDN_HEADS, GDN_DK, GDN_DV), F32), params)

    prompt_cfg = Cfg(batch=bp, nb=1, seq_len=PROMPT_TILE, valid_len=PROMPT_TILE, chunk=64,
                     n_tt=tp // PROMPT_TILE, shared_state=True)
    y_p, ta_p, tq_p, ssm_p = _layer_call(prompt_cfg, "layer_prompt", x_prompt, ta_m, tq_m, ssm_m, params)

    sample_cfg = Cfg(batch=bs, nb=SAMPLE_SEQS_PER_STEP, seq_len=SAMPLE_PAD_LEN, valid_len=ts, chunk=SAMPLE_PAD_LEN,
                     n_tt=1, shared_state=False)
    xs = jnp.pad(x_sample, ((0, 0), (0, SAMPLE_PAD_LEN - ts), (0, 0)))
    y_s, ta_s, tq_s, ssm_s = _layer_call(
        sample_cfg, "layer_sample", xs, state_conv_a[0], state_conv_qkv[0], state_ssm[0], params)

    return (y_p, y_s[:, :ts], ta_p[None], tq_p[None], ssm_p[None], ta_s[None], tq_s[None], ssm_s[None])
```

```python
import dataclasses
import functools

import jax
import jax.numpy as jnp
from jax import lax
from jax.experimental import pallas as pl
from jax.experimental.pallas import tpu as pltpu

F32 = jnp.float32
BF16 = jnp.bfloat16

D_MODEL = 1024
CONV_CH = 512
SCONV_K = 3
GDN_HEADS = 4
GDN_DK = 128
GDN_DV = 128
GDN_CONV_K = 4
QKV_W = GDN_HEADS * (2 * GDN_DK + GDN_DV)
FFN_HIDDEN = 4 * D_MODEL
EPS = 1e-6

O_B = 0
O_C = O_B + CONV_CH
O_H = O_C + CONV_CH
O_QKV = O_H + CONV_CH
O_G = O_QKV + QKV_W
O_AB = O_G + GDN_HEADS * GDN_DV
LANES = 128
SUBLANES = 8
IN_W_PAD = O_AB + LANES
VC_ROWS = 128
TAIL_PAD = SUBLANES
NEUMANN_BASE = 16
VMEM_LIMIT_BYTES = 56 * 1024 * 1024

Q_COL = 0
K_COL = GDN_HEADS * GDN_DK
V_COL = 2 * GDN_HEADS * GDN_DK


@dataclasses.dataclass(frozen=True)
class Cfg:
    batch: int
    nb: int
    seq_len: int
    valid_len: int
    chunk: int
    n_tt: int
    shared_state: bool

    @property
    def rows(self):
        return self.nb * self.seq_len

    @property
    def carry(self):
        return self.nb == 1


def _bf(x):
    return x.astype(BF16)


def _dot(a, b):
    return jnp.dot(a, b, preferred_element_type=F32)


def _dot_nt(a, b):
    return lax.dot_general(a, b, (((1,), (1,)), ((), ())), preferred_element_type=F32)


def _rms(x, g_row):
    return x * lax.rsqrt(jnp.mean(x * x, axis=-1, keepdims=True) + EPS) * g_row


def _softplus(x):
    return jnp.maximum(x, 0.0) + jnp.log1p(jnp.exp(-jnp.abs(x)))


def _silu(x):
    return x * jax.nn.sigmoid(x)


def _log2(n):
    k = n.bit_length() - 1
    assert (1 << k) == n, n
    return k


def _halves(x):
    return x[:, :LANES], x[:, LANES:]


def _pair(a, b):
    return jnp.concatenate([a, b], axis=1)


def _block_diag(y):
    ya, yb = _halves(y)
    z = jnp.zeros_like(ya)
    return jnp.concatenate([_pair(ya, z), _pair(z, yb)], axis=0)


def _pmm(x, y):
    return _dot(_bf(x), _block_diag(_bf(y)))


def _each(f, *lists):
    return [f(*xs) for xs in zip(*lists)]


def _problems(cfg):
    c = cfg.chunk
    if cfg.carry:
        assert 2 * c == VC_ROWS and GDN_HEADS == 4
        n = -(-cfg.valid_len // c)
        return [([(u * c, c, 0), (u * c, c, 2)], [(u * c, c, 1), (u * c, c, 3)]) for u in range(n)]
    half = VC_ROWS // GDN_HEADS
    assert half % cfg.seq_len == 0 and cfg.seq_len == c and cfg.rows % (2 * half) == 0
    return [([(u * 2 * half, half, h) for h in range(GDN_HEADS)],
             [(u * 2 * half + half, half, h) for h in range(GDN_HEADS)]) for u in range(cfg.rows // (2 * half))]


@dataclasses.dataclass
class Prep:
    u: jax.Array
    w: jax.Array
    qk: jax.Array
    q_dec: jax.Array
    k_dec: jax.Array
    g_last: tuple


def _delta_prep(cfg, problems, qkv_scr, gb_scr, gc_scr):
    c = cfg.chunk
    lc = _log2(c)
    ii = lax.broadcasted_iota(jnp.int32, (VC_ROWS, 2 * LANES), 0)
    jj = lax.broadcasted_iota(jnp.int32, (VC_ROWS, 2 * LANES), 1) & (LANES - 1)
    same = (ii >> lc) == (jj >> lc)
    lower = same & (jj <= ii)
    strict = same & (jj < ii)
    end_col = (jj == (ii | (c - 1)))[:, :LANES]
    eye = jnp.where(ii == jj, 1.0, 0.0)

    def stacked(ref, col, width, half):
        return jnp.concatenate([ref[r0:r0 + n, col(h):col(h) + width] for (r0, n, h) in half], axis=0)

    def packed(ref, base):
        return [_pair(*[stacked(ref, lambda h: base + LANES * h, LANES, half) for half in pr]) for pr in problems]

    def columns(ref, base):
        return [[stacked(ref, lambda h: base + h, 1, half) for half in pr] for pr in problems]

    def spread(cols):
        return _pair(*[jnp.broadcast_to(col, (VC_ROWS, LANES)) for col in cols])

    q, k, v = packed(qkv_scr, Q_COL), packed(qkv_scr, K_COL), packed(qkv_scr, V_COL)
    beta_c, gcum_c = columns(gb_scr, GDN_HEADS), columns(gc_scr, 0)
    beta, gcum = _each(spread, beta_c), _each(spread, gcum_c)
    g_rows = _each(lambda g: _pair(*[h.T for h in _halves(g)]), gcum)
    decay = _each(lambda g, gr: jnp.where(lower, jnp.exp(g - gr), 0.0), gcum, g_rows)
    g_last = _each(lambda gr: tuple(jnp.sum(jnp.where(end_col, h, 0.0), axis=1, keepdims=True) for h in _halves(gr)),
                   g_rows)
    kq = _each(lambda kk, qq: _dot_nt(_bf(jnp.concatenate([kk, qq], axis=0)), _block_diag(_bf(kk))), k, q)
    qk = _each(lambda m, d: m[VC_ROWS:] * d, kq, decay)
    a = _each(lambda m, b, d: jnp.where(strict, m[:VC_ROWS] * b * d, 0.0), kq, beta, decay)

    base = min(c, NEUMANN_BASE)
    lb = _log2(base)
    p = _each(lambda x: jnp.where((ii >> lb) == (jj >> lb), x, 0.0), a)
    t = _each(lambda x: eye - x, p)
    for _ in range(lb - 1):
        p = _each(_pmm, p, p)
        t = _each(lambda x, y: x + _pmm(x, y), t, p)
    size = base
    while size < c:
        ls = _log2(size)
        couple = ((ii >> (ls + 1)) == (jj >> (ls + 1))) & ((ii >> ls) != (jj >> ls))
        off = _each(lambda x: jnp.where(couple, x, 0.0), a)
        t_off = _each(_pmm, t, off)
        t = _each(lambda x, y: x - _pmm(y, x), t, t_off)
        size *= 2

    exp_g = _each(lambda cols: spread([jnp.exp(col) for col in cols]), gcum_c)
    k_scale = _each(lambda cols, gl: spread([jnp.exp(e - col) for col, e in zip(cols, gl)]), gcum_c, g_last)
    u = _each(lambda x, vv, b: _pmm(x, vv * b), t, v, beta)
    w = _each(lambda x, kk, b, e: _pmm(x, kk * (b * e)), t, k, beta, exp_g)
    q_dec = _each(lambda x, e: x * e, q, exp_g)
    k_dec = _each(lambda x, s: x * s, k, k_scale)
    return [Prep(*xs) for xs in zip(u, w, qk, q_dec, k_dec, g_last)]


def _delta_apply_carry(cfg, problems, preps, s_scr, o_scr):
    c = cfg.chunk
    top_rows = lax.broadcasted_iota(jnp.int32, (VC_ROWS, GDN_DV), 0) < c
    for (half_a, half_b), pr in zip(problems, preps):
        state = [s_scr[h] for h in range(GDN_HEADS)]
        w_s, q_s = [], []
        for blk in range(VC_ROWS // c):
            lo = blk * c
            wq = _bf(jnp.concatenate([pr.w[lo:lo + c], pr.q_dec[lo:lo + c]], axis=0))
            r = _dot(wq, _block_diag(_bf(_pair(state[half_a[blk][2]], state[half_b[blk][2]]))))
            w_s.append(r[:c])
            q_s.append(r[c:])
        v_new = pr.u - jnp.concatenate(w_s, axis=0)
        o = jnp.concatenate(q_s, axis=0) + _pmm(pr.qk, v_new)
        for half, vn, kd, gl, o_h in zip((half_a, half_b), _halves(v_new), _halves(pr.k_dec), pr.g_last, _halves(o)):
            upd = _dot(_bf(kd.T), _bf(_pair(jnp.where(top_rows, vn, 0.0), jnp.where(top_rows, 0.0, vn))))
            for blk, (r0, n, h) in enumerate(half):
                lo = blk * c
                s_scr[h] = state[h] * jnp.exp(gl[lo:lo + 1, :]) + upd[:, GDN_DV * blk:GDN_DV * (blk + 1)]
                o_scr[r0:r0 + n, GDN_DV * h:GDN_DV * (h + 1)] = o_h[lo:lo + c]


def _delta_apply_blocks(cfg, problems, preps, ssm_ref, nssm_ref, o_scr):
    c = cfg.chunk
    lc = _log2(c)
    blk_id = lax.broadcasted_iota(jnp.int32, (VC_ROWS, GDN_DV), 0) >> lc
    for pieces, pr in zip(problems, preps):
        halves = []
        for half, w_h, qd_h, u_h in zip(pieces, _halves(pr.w), _halves(pr.q_dec), _halves(pr.u)):
            blocks = [(r0 + t0, h, (r0 + t0) // cfg.seq_len) for (r0, n, h) in half for t0 in range(0, n, c)]
            vs, qs = [], []
            for b, (_, h, seq) in enumerate(blocks):
                lo = b * c
                r = _dot(_bf(jnp.concatenate([w_h[lo:lo + c], qd_h[lo:lo + c]], axis=0)), _bf(ssm_ref[seq, h]))
                vs.append(u_h[lo:lo + c] - r[:c])
                qs.append(r[c:])
            halves.append((blocks, jnp.concatenate(vs, axis=0), jnp.concatenate(qs, axis=0)))
        v_new = _pair(halves[0][1], halves[1][1])
        o = _pair(halves[0][2], halves[1][2]) + _pmm(pr.qk, v_new)
        for (blocks, vn, _), kd, gl, o_h in zip(halves, _halves(pr.k_dec), pr.g_last, _halves(o)):
            kd_t = _bf(kd.T)
            for b0 in range(0, len(blocks), 2):
                upd = _dot(kd_t, _bf(_pair(jnp.where(blk_id == b0, vn, 0.0), jnp.where(blk_id == b0 + 1, vn, 0.0))))
                for b in (b0, b0 + 1):
                    row, h, seq = blocks[b]
                    lo = b * c
                    nssm_ref[seq, h] = (ssm_ref[seq, h] * jnp.exp(gl[lo:lo + 1, :])
                                        + upd[:, GDN_DV * (b - b0):GDN_DV * (b - b0 + 1)])
                    o_scr[row:row + c, GDN_DV * h:GDN_DV * (h + 1)] = o_h[lo:lo + c]


def _layer_kernel(cfg, x_ref, ta_ref, tq_ref, ssm_ref, nmix_ref, win_ref, caw_ref, cqw_ref, alog_ref, dt_ref,
                  gnorm_ref, wout_ref, nffn_ref, wup_ref, wdown_ref, nfin_ref,
                  y_ref, nta_ref, ntq_ref, nssm_ref,
                  z_scr, xa_buf, xq_buf, qkv_scr, gb_scr, gc_scr, o_scr, s_scr):
    j = pl.program_id(1)
    nb, sl, rows = cfg.nb, cfg.seq_len, cfg.rows
    first = j == 0
    last = j == cfg.n_tt - 1

    @pl.when(first)
    def _():
        xa_buf[:, TAIL_PAD - (SCONV_K - 1):TAIL_PAD, :] = ta_ref[...]
        xq_buf[:, TAIL_PAD - (GDN_CONV_K - 1):TAIL_PAD, :] = tq_ref[...]
        if cfg.carry:
            s_scr[...] = ssm_ref[0]

    x = x_ref[...].reshape(rows, D_MODEL)
    xn = _rms(x, nmix_ref[...])
    z_scr[...] = _dot(_bf(xn), win_ref[...])

    xa_buf[:, TAIL_PAD:, :] = (z_scr[:, O_C:O_H] * z_scr[:, O_H:O_QKV]).reshape(nb, sl, CONV_CH)
    conv_a = None
    for tap in range(SCONV_K):
        lo = TAIL_PAD - (SCONV_K - 1) + tap
        term = xa_buf[:, lo:lo + sl, :] * caw_ref[tap:tap + 1, :]
        conv_a = term if conv_a is None else conv_a + term
    y_a = z_scr[:, O_B:O_C] * conv_a.reshape(rows, CONV_CH)

    xq_buf[:, TAIL_PAD:, :] = z_scr[:, O_QKV:O_G].reshape(nb, sl, QKV_W)
    conv_q = None
    for tap in range(GDN_CONV_K):
        lo = TAIL_PAD - (GDN_CONV_K - 1) + tap
        term = xq_buf[:, lo:lo + sl, :] * cqw_ref[tap:tap + 1, :]
        conv_q = term if conv_q is None else conv_q + term
    qkv = _silu(conv_q.reshape(rows, QKV_W))
    for h in range(GDN_HEADS):
        qh = qkv[:, Q_COL + GDN_DK * h:Q_COL + GDN_DK * (h + 1)]
        kh = qkv[:, K_COL + GDN_DK * h:K_COL + GDN_DK * (h + 1)]
        qkv_scr[:, Q_COL + GDN_DK * h:Q_COL + GDN_DK * (h + 1)] = (
            qh * lax.rsqrt(jnp.sum(qh * qh, axis=-1, keepdims=True) + EPS) * (GDN_DK ** -0.5))
        qkv_scr[:, K_COL + GDN_DK * h:K_COL + GDN_DK * (h + 1)] = (
            kh * lax.rsqrt(jnp.sum(kh * kh, axis=-1, keepdims=True) + EPS))
    qkv_scr[:, V_COL:] = qkv[:, V_COL:]

    @pl.when(last)
    def _():
        nta_ref[...] = xa_buf[:, TAIL_PAD + cfg.valid_len - (SCONV_K - 1):TAIL_PAD + cfg.valid_len, :]
        ntq_ref[...] = xq_buf[:, TAIL_PAD + cfg.valid_len - (GDN_CONV_K - 1):TAIL_PAD + cfg.valid_len, :]

    if cfg.n_tt > 1:
        xa_buf[:, TAIL_PAD - (SCONV_K - 1):TAIL_PAD, :] = xa_buf[:, TAIL_PAD + sl - (SCONV_K - 1):TAIL_PAD + sl, :]
        xq_buf[:, TAIL_PAD - (GDN_CONV_K - 1):TAIL_PAD, :] = xq_buf[:, TAIL_PAD + sl - (GDN_CONV_K - 1):TAIL_PAD + sl, :]

    zab = z_scr[:, O_AB:IN_W_PAD]
    g = -jnp.exp(alog_ref[...]) * _softplus(zab + dt_ref[...])
    beta = jax.nn.sigmoid(zab)
    lane = lax.broadcasted_iota(jnp.int32, (rows, LANES), 1)
    gb = jnp.where(lane < GDN_HEADS, g, beta)
    if cfg.valid_len < sl:
        step = lax.broadcasted_iota(jnp.int32, (rows, LANES), 0) & (sl - 1)
        gb = jnp.where(step < cfg.valid_len, gb, 0.0)
    gb_scr[...] = gb
    ri = lax.broadcasted_iota(jnp.int32, (rows, rows), 0)
    rj = lax.broadcasted_iota(jnp.int32, (rows, rows), 1)
    lc = _log2(cfg.chunk)
    tri = jnp.where(((ri >> lc) == (rj >> lc)) & (rj <= ri), 1.0, 0.0).astype(BF16)
    g1 = _bf(gb)
    r1 = gb - g1.astype(F32)
    g2 = _bf(r1)
    g3 = _bf(r1 - g2.astype(F32))
    gc_scr[...] = _dot(tri, g1) + (_dot(tri, g2) + _dot(tri, g3))

    problems = _problems(cfg)
    preps = _delta_prep(cfg, problems, qkv_scr, gb_scr, gc_scr)
    if cfg.carry:
        done = len(problems) * cfg.chunk
        if done < sl:
            o_scr[done:, :] = jnp.zeros((sl - done, GDN_HEADS * GDN_DV), F32)
        _delta_apply_carry(cfg, problems, preps, s_scr, o_scr)

        @pl.when(last)
        def _():
            nssm_ref[0] = s_scr[...]
    else:
        _delta_apply_blocks(cfg, problems, preps, ssm_ref, nssm_ref, o_scr)

    parts = [y_a]
    for h in range(GDN_HEADS):
        oh = o_scr[:, GDN_DV * h:GDN_DV * (h + 1)]
        gate = z_scr[:, O_G + GDN_DV * h:O_G + GDN_DV * (h + 1)]
        parts.append(_rms(oh, gnorm_ref[...]) * _silu(gate))
    mix = jnp.concatenate(parts, axis=1)
    h1 = x + _dot(_bf(mix), wout_ref[...])

    up = jnp.maximum(_dot(_bf(_rms(h1, nffn_ref[...])), wup_ref[...]), 0.0)
    h2 = h1 + _dot(_bf(up * up), wdown_ref[...])
    y_ref[...] = _rms(h2, nfin_ref[...]).reshape(nb, sl, D_MODEL)


def _layer_call(cfg, name, x, tail_a, tail_q, ssm, params):
    nb, sl, rows = cfg.nb, cfg.seq_len, cfg.rows
    assert x.shape == (cfg.batch, sl * cfg.n_tt, D_MODEL), (x.shape, cfg)
    assert cfg.batch % nb == 0 and sl % SUBLANES == 0 and rows % VC_ROWS == 0
    assert not cfg.shared_state or nb == 1
    n_bblk = cfg.batch // nb

    if cfg.shared_state:
        def state_map(i, j):
            return (0,)
    else:
        def state_map(i, j):
            return (i,)

    def state_spec(shape):
        zeros = (0,) * len(shape)
        return pl.BlockSpec((nb,) + shape, lambda i, j: state_map(i, j) + zeros)

    def out_state_spec(shape):
        zeros = (0,) * len(shape)
        return pl.BlockSpec((nb,) + shape, lambda i, j: (i,) + zeros)

    def const_spec(arr):
        zeros = (0,) * arr.ndim
        return pl.BlockSpec(arr.shape, lambda i, j: zeros, pipeline_mode=pl.Buffered(1))

    ta_shape, tq_shape = (SCONV_K - 1, CONV_CH), (GDN_CONV_K - 1, QKV_W)
    ssm_shape = (GDN_HEADS, GDN_DK, GDN_DV)
    x_spec = pl.BlockSpec((nb, sl, D_MODEL), lambda i, j: (i, j, 0))
    in_specs = [x_spec, state_spec(ta_shape), state_spec(tq_shape), state_spec(ssm_shape)]
    in_specs += [const_spec(p) for p in params]
    out_specs = [x_spec, out_state_spec(ta_shape), out_state_spec(tq_shape), out_state_spec(ssm_shape)]
    out_shape = [
        jax.ShapeDtypeStruct(x.shape, F32),
        jax.ShapeDtypeStruct((cfg.batch,) + ta_shape, F32),
        jax.ShapeDtypeStruct((cfg.batch,) + tq_shape, F32),
        jax.ShapeDtypeStruct((cfg.batch,) + ssm_shape, F32),
    ]
    scratch = [
        pltpu.VMEM((rows, IN_W_PAD), F32),
        pltpu.VMEM((nb, TAIL_PAD + sl, CONV_CH), F32),
        pltpu.VMEM((nb, TAIL_PAD + sl, QKV_W), F32),
        pltpu.VMEM((rows, QKV_W), F32),
        pltpu.VMEM((rows, LANES), F32),
        pltpu.VMEM((rows, LANES), F32),
        pltpu.VMEM((rows, GDN_HEADS * GDN_DV), F32),
        pltpu.VMEM(ssm_shape, F32),
    ]
    return pl.pallas_call(
        functools.partial(_layer_kernel, cfg),
        grid=(n_bblk, cfg.n_tt),
        in_specs=in_specs,
        out_specs=out_specs,
        out_shape=out_shape,
        scratch_shapes=scratch,
        compiler_params=pltpu.CompilerParams(
            dimension_semantics=("arbitrary", "arbitrary"), vmem_limit_bytes=VMEM_LIMIT_BYTES),
        name=name,
    )(x, tail_a, tail_q, ssm, *params)


META_PAD_LEN = 128
PROMPT_TILE = 256
SAMPLE_PAD_LEN = 8
SAMPLE_SEQS_PER_STEP = 16


def kernel(x_prompt, x_sample, state_conv_a, state_conv_qkv, state_ssm, meta_tokens, norm_mix, w_in, conv_a_w,
           conv_qkv_w, a_log, dt_bias, gdn_norm, w_out, norm_ffn, w_up, w_down, norm_final):
    assert norm_mix.shape[0] == 1, "single layer"
    bp, tp, _ = x_prompt.shape
    bs, ts, _ = x_sample.shape
    n_meta = meta_tokens.shape[0]

    def row(v):
        return v.reshape(1, -1).astype(F32)

    def lane_row(v):
        return jnp.pad(v.astype(F32), (0, LANES - v.shape[0])).reshape(1, LANES)

    params = (
        row(norm_mix[0]),
        _bf(jnp.pad(w_in[0], ((0, 0), (0, IN_W_PAD - w_in.shape[2])))),
        conv_a_w[0].astype(F32), conv_qkv_w[0].astype(F32),
        lane_row(a_log[0]), lane_row(dt_bias[0]), row(gdn_norm[0]),
        _bf(w_out[0]), row(norm_ffn[0]), _bf(w_up[0]), _bf(w_down[0]), row(norm_final),
    )

    meta_cfg = Cfg(batch=1, nb=1, seq_len=META_PAD_LEN, valid_len=n_meta, chunk=64, n_tt=1, shared_state=False)
    meta_x = jnp.pad(meta_tokens.astype(F32), ((0, META_PAD_LEN - n_meta), (0, 0)))[None]
    _, ta_m, tq_m, ssm_m = _layer_call(
        meta_cfg, "layer_meta", meta_x,
        jnp.zeros((1, SCONV_K - 1, CONV_CH), F32), jnp.zeros((1, GDN_CONV_K - 1, QKV_W), F32),
        jnp.zeros((1, GDN_HEADS, GDN_DK, GDN_DV), F32), params)

    prompt_cfg = Cfg(batch=bp, nb=1, seq_len=PROMPT_TILE, valid_len=PROMPT_TILE, chunk=64,
                     n_tt=tp // PROMPT_TILE, shared_state=True)
    y_p, ta_p, tq_p, ssm_p = _layer_call(prompt_cfg, "layer_prompt", x_prompt, ta_m, tq_m, ssm_m, params)

    sample_cfg = Cfg(batch=bs, nb=SAMPLE_SEQS_PER_STEP, seq_len=SAMPLE_PAD_LEN, valid_len=ts, chunk=SAMPLE_PAD_LEN,
                     n_tt=1, shared_state=False)
    xs = jnp.pad(x_sample, ((0, 0), (0, SAMPLE_PAD_LEN - ts), (0, 0)))
    y_s, ta_s, tq_s, ssm_s = _layer_call(
        sample_cfg, "layer_sample", xs, state_conv_a[0], state_conv_qkv[0], state_ssm[0], params)

    return (y_p, y_s[:, :ts], ta_p[None], tq_p[None], ssm_p[None], ta_s[None], tq_s[None], ssm_s[None])
```

```python
import dataclasses
import functools

import jax
import jax.numpy as jnp
from jax import lax
from jax.experimental import pallas as pl
from jax.experimental.pallas import tpu as pltpu

F32 = jnp.float32
BF16 = jnp.bfloat16

D_MODEL = 1024
CONV_CH = 512
SCONV_K = 3
GDN_HEADS = 4
GDN_DK = 128
GDN_DV = 128
GDN_CONV_K = 4
QKV_W = GDN_HEADS * (2 * GDN_DK + GDN_DV)
FFN_HIDDEN = 4 * D_MODEL
EPS = 1e-6

O_B = 0
O_C = O_B + CONV_CH
O_H = O_C + CONV_CH
O_QKV = O_H + CONV_CH
O_G = O_QKV + QKV_W
O_AB = O_G + GDN_HEADS * GDN_DV
LANES = 128
SUBLANES = 8
IN_W_PAD = O_AB + LANES
VC_ROWS = 128
TAIL_PAD = SUBLANES
NEUMANN_BASE = 16
FFN_CHUNK = 256
VMEM_LIMIT_BYTES = 56 * 1024 * 1024

Q_COL = 0
K_COL = GDN_HEADS * GDN_DK
V_COL = 2 * GDN_HEADS * GDN_DK


@dataclasses.dataclass(frozen=True)
class Cfg:
    batch: int
    nb: int
    seq_len: int
    valid_len: int
    chunk: int
    n_tt: int
    shared_state: bool
    pipelined: bool

    @property
    def rows(self):
        return self.nb * self.seq_len

    @property
    def n_tiles(self):
        return (self.batch // self.nb) * self.n_tt

    @property
    def carry(self):
        return self.nb == 1


def _bf(x):
    return x.astype(BF16)


def _dot(a, b):
    return jnp.dot(a, b, preferred_element_type=F32)


def _dot_nt(a, b):
    return lax.dot_general(a, b, (((1,), (1,)), ((), ())), preferred_element_type=F32)


def _rms(x, g_row):
    return x * lax.rsqrt(jnp.mean(x * x, axis=-1, keepdims=True) + EPS) * g_row


def _softplus(x):
    return jnp.maximum(x, 0.0) + jnp.log1p(jnp.exp(-jnp.abs(x)))


def _silu(x):
    return x * jax.nn.sigmoid(x)


def _log2(n):
    k = n.bit_length() - 1
    assert (1 << k) == n, n
    return k


def _halves(x):
    return x[:, :LANES], x[:, LANES:]


def _pair(a, b):
    return jnp.concatenate([a, b], axis=1)


def _block_diag(y):
    ya, yb = _halves(y)
    z = jnp.zeros_like(ya)
    return jnp.concatenate([_pair(ya, z), _pair(z, yb)], axis=0)


def _pmm(x, y):
    return _dot(_bf(x), _block_diag(_bf(y)))


def _each(f, *lists):
    return [f(*xs) for xs in zip(*lists)]


def _problems(cfg):
    c = cfg.chunk
    if cfg.carry:
        assert 2 * c == VC_ROWS and GDN_HEADS == 4
        n = -(-cfg.valid_len // c)
        return [([(u * c, c, 0), (u * c, c, 2)], [(u * c, c, 1), (u * c, c, 3)]) for u in range(n)]
    half = VC_ROWS // GDN_HEADS
    assert half % cfg.seq_len == 0 and cfg.seq_len == c and cfg.rows % (2 * half) == 0
    return [([(u * 2 * half, half, h) for h in range(GDN_HEADS)],
             [(u * 2 * half + half, half, h) for h in range(GDN_HEADS)]) for u in range(cfg.rows // (2 * half))]


@dataclasses.dataclass
class Prep:
    u: jax.Array
    w: jax.Array
    qk: jax.Array
    q_dec: jax.Array
    k_dec: jax.Array
    g_last: tuple


def _delta_prep(cfg, problems, qkv_scr, gb_scr, gc_scr, out):
    c = cfg.chunk
    lc = _log2(c)
    ii = lax.broadcasted_iota(jnp.int32, (VC_ROWS, 2 * LANES), 0)
    jj = lax.broadcasted_iota(jnp.int32, (VC_ROWS, 2 * LANES), 1) & (LANES - 1)
    same = (ii >> lc) == (jj >> lc)
    lower = same & (jj <= ii)
    strict = same & (jj < ii)
    end_col = (jj == (ii | (c - 1)))[:, :LANES]
    eye = jnp.where(ii == jj, 1.0, 0.0)

    def stacked(ref, col, width, half):
        return jnp.concatenate([ref[r0:r0 + n, col(h):col(h) + width] for (r0, n, h) in half], axis=0)

    def packed(ref, base):
        return [_pair(*[stacked(ref, lambda h: base + LANES * h, LANES, half) for half in pr]) for pr in problems]

    def columns(ref, base):
        return [[stacked(ref, lambda h: base + h, 1, half) for half in pr] for pr in problems]

    def spread(cols):
        return _pair(*[jnp.broadcast_to(col, (VC_ROWS, LANES)) for col in cols])

    q, k, v = packed(qkv_scr, Q_COL), packed(qkv_scr, K_COL), packed(qkv_scr, V_COL)
    beta_c, gcum_c = columns(gb_scr, GDN_HEADS), columns(gc_scr, 0)
    beta, gcum = _each(spread, beta_c), _each(spread, gcum_c)
    g_rows = _each(lambda g: _pair(*[h.T for h in _halves(g)]), gcum)
    decay = _each(lambda g, gr: jnp.where(lower, jnp.exp(g - gr), 0.0), gcum, g_rows)
    g_last = _each(lambda gr: tuple(jnp.sum(jnp.where(end_col, h, 0.0), axis=1, keepdims=True) for h in _halves(gr)),
                   g_rows)
    kq = _each(lambda kk, qq: _dot_nt(_bf(jnp.concatenate([kk, qq], axis=0)), _block_diag(_bf(kk))), k, q)
    qk = _each(lambda m, d: m[VC_ROWS:] * d, kq, decay)
    a = _each(lambda m, b, d: jnp.where(strict, m[:VC_ROWS] * b * d, 0.0), kq, beta, decay)
    yield

    base = min(c, NEUMANN_BASE)
    lb = _log2(base)
    p = _each(lambda x: jnp.where((ii >> lb) == (jj >> lb), x, 0.0), a)
    t = _each(lambda x: eye - x, p)
    for _ in range(lb - 1):
        p = _each(_pmm, p, p)
        t = _each(lambda x, y: x + _pmm(x, y), t, p)
        yield
    size = base
    while size < c:
        ls = _log2(size)
        couple = ((ii >> (ls + 1)) == (jj >> (ls + 1))) & ((ii >> ls) != (jj >> ls))
        off = _each(lambda x: jnp.where(couple, x, 0.0), a)
        t_off = _each(_pmm, t, off)
        t = _each(lambda x, y: x - _pmm(y, x), t, t_off)
        size *= 2
        yield

    exp_g = _each(lambda cols: spread([jnp.exp(col) for col in cols]), gcum_c)
    k_scale = _each(lambda cols, gl: spread([jnp.exp(e - col) for col, e in zip(cols, gl)]), gcum_c, g_last)
    u = _each(lambda x, vv, b: _pmm(x, vv * b), t, v, beta)
    w = _each(lambda x, kk, b, e: _pmm(x, kk * (b * e)), t, k, beta, exp_g)
    q_dec = _each(lambda x, e: x * e, q, exp_g)
    k_dec = _each(lambda x, s: x * s, k, k_scale)
    out["preps"] = [Prep(*xs) for xs in zip(u, w, qk, q_dec, k_dec, g_last)]


def _delta_apply_carry(cfg, problems, preps, s_scr, o_scr):
    c = cfg.chunk
    top_rows = lax.broadcasted_iota(jnp.int32, (VC_ROWS, GDN_DV), 0) < c
    for (half_a, half_b), pr in zip(problems, preps):
        state = [s_scr[h] for h in range(GDN_HEADS)]
        w_s, q_s = [], []
        for blk in range(VC_ROWS // c):
            lo = blk * c
            wq = _bf(jnp.concatenate([pr.w[lo:lo + c], pr.q_dec[lo:lo + c]], axis=0))
            r = _dot(wq, _block_diag(_bf(_pair(state[half_a[blk][2]], state[half_b[blk][2]]))))
            w_s.append(r[:c])
            q_s.append(r[c:])
        v_new = pr.u - jnp.concatenate(w_s, axis=0)
        o = jnp.concatenate(q_s, axis=0) + _pmm(pr.qk, v_new)
        for half, vn, kd, gl, o_h in zip((half_a, half_b), _halves(v_new), _halves(pr.k_dec), pr.g_last, _halves(o)):
            upd = _dot(_bf(kd.T), _bf(_pair(jnp.where(top_rows, vn, 0.0), jnp.where(top_rows, 0.0, vn))))
            for blk, (r0, n, h) in enumerate(half):
                lo = blk * c
                s_scr[h] = state[h] * jnp.exp(gl[lo:lo + 1, :]) + upd[:, GDN_DV * blk:GDN_DV * (blk + 1)]
                o_scr[r0:r0 + n, GDN_DV * h:GDN_DV * (h + 1)] = o_h[lo:lo + c]


def _delta_apply_blocks(cfg, problems, preps, ssm_ref, nssm_ref, o_scr):
    c = cfg.chunk
    lc = _log2(c)
    blk_id = lax.broadcasted_iota(jnp.int32, (VC_ROWS, GDN_DV), 0) >> lc
    for pieces, pr in zip(problems, preps):
        halves = []
        for half, w_h, qd_h, u_h in zip(pieces, _halves(pr.w), _halves(pr.q_dec), _halves(pr.u)):
            blocks = [(r0 + t0, h, (r0 + t0) // cfg.seq_len) for (r0, n, h) in half for t0 in range(0, n, c)]
            vs, qs = [], []
            for b, (_, h, seq) in enumerate(blocks):
                lo = b * c
                r = _dot(_bf(jnp.concatenate([w_h[lo:lo + c], qd_h[lo:lo + c]], axis=0)), _bf(ssm_ref[seq, h]))
                vs.append(u_h[lo:lo + c] - r[:c])
                qs.append(r[c:])
            halves.append((blocks, jnp.concatenate(vs, axis=0), jnp.concatenate(qs, axis=0)))
        v_new = _pair(halves[0][1], halves[1][1])
        o = _pair(halves[0][2], halves[1][2]) + _pmm(pr.qk, v_new)
        for (blocks, vn, _), kd, gl, o_h in zip(halves, _halves(pr.k_dec), pr.g_last, _halves(o)):
            kd_t = _bf(kd.T)
            for b0 in range(0, len(blocks), 2):
                upd = _dot(kd_t, _bf(_pair(jnp.where(blk_id == b0, vn, 0.0), jnp.where(blk_id == b0 + 1, vn, 0.0))))
                for b in (b0, b0 + 1):
                    row, h, seq = blocks[b]
                    lo = b * c
                    nssm_ref[seq, h] = (ssm_ref[seq, h] * jnp.exp(gl[lo:lo + 1, :])
                                        + upd[:, GDN_DV * (b - b0):GDN_DV * (b - b0 + 1)])
                    o_scr[row:row + c, GDN_DV * h:GDN_DV * (h + 1)] = o_h[lo:lo + c]


def _layer_kernel(cfg, x_ref, xb_ref, ta_ref, tq_ref, ssm_ref, nmix_ref, win_ref, caw_ref, cqw_ref, alog_ref, dt_ref,
                  gnorm_ref, wout_ref, nffn_ref, wup_ref, wdown_ref, nfin_ref,
                  y_ref, nta_ref, ntq_ref, nssm_ref,
                  z_scr, xa_buf, xq_buf, qkv_scr, gb_scr, gc_scr, o_scr, s_scr, mix_scr):
    step = pl.program_id(0)
    nb, sl, rows = cfg.nb, cfg.seq_len, cfg.rows
    if cfg.pipelined:
        tile = jnp.minimum(step, cfg.n_tiles - 1)
        real = step < cfg.n_tiles
    else:
        tile = step
        real = True
    j = lax.rem(tile, cfg.n_tt)
    first = j == 0
    last = (j == cfg.n_tt - 1) & real

    @pl.when(first)
    def _():
        xa_buf[:, TAIL_PAD - (SCONV_K - 1):TAIL_PAD, :] = ta_ref[...]
        xq_buf[:, TAIL_PAD - (GDN_CONV_K - 1):TAIL_PAD, :] = tq_ref[...]
        if cfg.carry:
            s_scr[...] = ssm_ref[0]

    if cfg.pipelined:
        @pl.when(step == 0)
        def _():
            mix_scr[...] = jnp.zeros(mix_scr.shape, mix_scr.dtype)

    st = {}
    problems = _problems(cfg)
    ffn_chunk = FFN_CHUNK if cfg.pipelined else FFN_HIDDEN

    def proj(lo, hi):
        z_scr[:, lo:hi] = _dot(st["xn"], win_ref[:, lo:hi])

    def conv_q(col):
        width = GDN_HEADS * GDN_DK
        xq_buf[:, TAIL_PAD:, col:col + width] = z_scr[:, O_QKV + col:O_QKV + col + width].reshape(nb, sl, width)
        acc = None
        for tap in range(GDN_CONV_K):
            lo = TAIL_PAD - (GDN_CONV_K - 1) + tap
            term = xq_buf[:, lo:lo + sl, col:col + width] * cqw_ref[tap:tap + 1, col:col + width]
            acc = term if acc is None else acc + term
        act = _silu(acc.reshape(rows, width))
        if col == V_COL:
            qkv_scr[:, col:col + width] = act
            return
        scale = GDN_DK ** -0.5 if col == Q_COL else 1.0
        for h in range(GDN_HEADS):
            ah = act[:, GDN_DK * h:GDN_DK * (h + 1)]
            qkv_scr[:, col + GDN_DK * h:col + GDN_DK * (h + 1)] = (
                ah * (lax.rsqrt(jnp.sum(ah * ah, axis=-1, keepdims=True) + EPS) * scale))

    def conv_a():
        xa_buf[:, TAIL_PAD:, :] = (z_scr[:, O_C:O_H] * z_scr[:, O_H:O_QKV]).reshape(nb, sl, CONV_CH)
        acc = None
        for tap in range(SCONV_K):
            lo = TAIL_PAD - (SCONV_K - 1) + tap
            term = xa_buf[:, lo:lo + sl, :] * caw_ref[tap:tap + 1, :]
            acc = term if acc is None else acc + term
        mix_scr[:, :CONV_CH] = _bf(z_scr[:, O_B:O_C] * acc.reshape(rows, CONV_CH))

    def gates():
        zab = z_scr[:, O_AB:IN_W_PAD]
        g = -jnp.exp(alog_ref[...]) * _softplus(zab + dt_ref[...])
        beta = jax.nn.sigmoid(zab)
        lane = lax.broadcasted_iota(jnp.int32, (rows, LANES), 1)
        gb = jnp.where(lane < GDN_HEADS, g, beta)
        if cfg.valid_len < sl:
            t_in_seq = lax.broadcasted_iota(jnp.int32, (rows, LANES), 0) & (sl - 1)
            gb = jnp.where(t_in_seq < cfg.valid_len, gb, 0.0)
        gb_scr[...] = gb
        ri = lax.broadcasted_iota(jnp.int32, (rows, rows), 0)
        rj = lax.broadcasted_iota(jnp.int32, (rows, rows), 1)
        lc = _log2(cfg.chunk)
        tri = jnp.where(((ri >> lc) == (rj >> lc)) & (rj <= ri), 1.0, 0.0).astype(BF16)
        g1 = _bf(gb)
        r1 = gb - g1.astype(F32)
        g2 = _bf(r1)
        g3 = _bf(r1 - g2.astype(F32))
        gc_scr[...] = _dot(tri, g1) + (_dot(tri, g2) + _dot(tri, g3))

    def gated_norm():
        for h in range(GDN_HEADS):
            oh = o_scr[:, GDN_DV * h:GDN_DV * (h + 1)]
            gate = z_scr[:, O_G + GDN_DV * h:O_G + GDN_DV * (h + 1)]
            mix_scr[:, CONV_CH + GDN_DV * h:CONV_CH + GDN_DV * (h + 1)] = _bf(_rms(oh, gnorm_ref[...]) * _silu(gate))

    def front_half():
        st["xn"] = _bf(_rms(x_ref[...].reshape(rows, D_MODEL), nmix_ref[...]))
        proj(O_QKV, O_G)
        yield 0
        proj(O_B, O_QKV)
        yield 0
        conv_q(Q_COL)
        yield 0
        proj(O_G, IN_W_PAD)
        yield 0
        conv_q(K_COL)
        yield 3
        conv_q(V_COL)
        yield 2
        conv_a()
        yield 3
        gates()
        yield 2
        for i, _ in enumerate(_delta_prep(cfg, problems, qkv_scr, gb_scr, gc_scr, st)):
            yield 2 if i == 0 else 1
        done = len(problems) * cfg.chunk
        if cfg.carry and done < sl:
            o_scr[done:, :] = jnp.zeros((sl - done, GDN_HEADS * GDN_DV), F32)
        yield 2
        for i in range(len(problems)):
            if cfg.carry:
                _delta_apply_carry(cfg, problems[i:i + 1], st["preps"][i:i + 1], s_scr, o_scr)
            else:
                _delta_apply_blocks(cfg, problems[i:i + 1], st["preps"][i:i + 1], ssm_ref, nssm_ref, o_scr)
            yield 3
        gated_norm()
        yield 1

    def back_half():
        h1 = xb_ref[...].reshape(rows, D_MODEL) + _dot(mix_scr[...], wout_ref[...])
        hn = _bf(_rms(h1, nffn_ref[...]))
        acc = h1
        yield
        for lo in range(0, FFN_HIDDEN, ffn_chunk):
            up = jnp.maximum(_dot(hn, wup_ref[:, lo:lo + ffn_chunk]), 0.0)
            yield
            acc = acc + _dot(_bf(up * up), wdown_ref[lo:lo + ffn_chunk, :])
            yield
        y_ref[...] = _rms(acc, nfin_ref[...]).reshape(nb, sl, D_MODEL)

    back = back_half()
    if cfg.pipelined:
        next(back)
        for n_back in front_half():
            for _ in range(n_back):
                next(back, None)
    else:
        for _ in front_half():
            pass
    for _ in back:
        pass

    if cfg.n_tt > 1:
        xa_buf[:, TAIL_PAD - (SCONV_K - 1):TAIL_PAD, :] = xa_buf[:, TAIL_PAD + sl - (SCONV_K - 1):TAIL_PAD + sl, :]
        xq_buf[:, TAIL_PAD - (GDN_CONV_K - 1):TAIL_PAD, :] = xq_buf[:, TAIL_PAD + sl - (GDN_CONV_K - 1):TAIL_PAD + sl, :]

    @pl.when(last)
    def _():
        nta_ref[...] = xa_buf[:, TAIL_PAD + cfg.valid_len - (SCONV_K - 1):TAIL_PAD + cfg.valid_len, :]
        ntq_ref[...] = xq_buf[:, TAIL_PAD + cfg.valid_len - (GDN_CONV_K - 1):TAIL_PAD + cfg.valid_len, :]
        if cfg.carry:
            nssm_ref[0] = s_scr[...]


def _layer_call(cfg, name, x, tail_a, tail_q, ssm, params):
    nb, sl, rows = cfg.nb, cfg.seq_len, cfg.rows
    assert x.shape == (cfg.batch, sl * cfg.n_tt, D_MODEL), (x.shape, cfg)
    assert cfg.batch % nb == 0 and sl % SUBLANES == 0 and rows % VC_ROWS == 0
    assert not cfg.shared_state or nb == 1
    n_tiles, n_tt = cfg.n_tiles, cfg.n_tt

    def front_tile(s):
        return jnp.minimum(s, n_tiles - 1) if cfg.pipelined else s

    def back_tile(s):
        return jnp.maximum(s - 1, 0) if cfg.pipelined else s

    def tile_spec(which):
        return pl.BlockSpec((nb, sl, D_MODEL), lambda s: (which(s) // n_tt, which(s) % n_tt, 0))

    def state_spec(shape):
        zeros = (0,) * len(shape)
        if cfg.shared_state:
            return pl.BlockSpec((nb,) + shape, lambda s: (0,) + zeros)
        return pl.BlockSpec((nb,) + shape, lambda s: (front_tile(s) // n_tt,) + zeros)

    def out_state_spec(shape):
        zeros = (0,) * len(shape)
        return pl.BlockSpec((nb,) + shape, lambda s: (front_tile(s) // n_tt,) + zeros)

    def const_spec(arr):
        zeros = (0,) * arr.ndim
        return pl.BlockSpec(arr.shape, lambda s: zeros, pipeline_mode=pl.Buffered(1))

    ta_shape, tq_shape = (SCONV_K - 1, CONV_CH), (GDN_CONV_K - 1, QKV_W)
    ssm_shape = (GDN_HEADS, GDN_DK, GDN_DV)
    in_specs = [tile_spec(front_tile), tile_spec(back_tile),
                state_spec(ta_shape), state_spec(tq_shape), state_spec(ssm_shape)]
    in_specs += [const_spec(p) for p in params]
    out_specs = [tile_spec(back_tile), out_state_spec(ta_shape), out_state_spec(tq_shape), out_state_spec(ssm_shape)]
    out_shape = [
        jax.ShapeDtypeStruct(x.shape, F32),
        jax.ShapeDtypeStruct((cfg.batch,) + ta_shape, F32),
        jax.ShapeDtypeStruct((cfg.batch,) + tq_shape, F32),
        jax.ShapeDtypeStruct((cfg.batch,) + ssm_shape, F32),
    ]
    scratch = [
        pltpu.VMEM((rows, IN_W_PAD), F32),
        pltpu.VMEM((nb, TAIL_PAD + sl, CONV_CH), F32),
        pltpu.VMEM((nb, TAIL_PAD + sl, QKV_W), F32),
        pltpu.VMEM((rows, QKV_W), F32),
        pltpu.VMEM((rows, LANES), F32),
        pltpu.VMEM((rows, LANES), F32),
        pltpu.VMEM((rows, GDN_HEADS * GDN_DV), F32),
        pltpu.VMEM(ssm_shape, F32),
        pltpu.VMEM((rows, D_MODEL), BF16),
    ]
    return pl.pallas_call(
        functools.partial(_layer_kernel, cfg),
        grid=(n_tiles + (1 if cfg.pipelined else 0),),
        in_specs=in_specs,
        out_specs=out_specs,
        out_shape=out_shape,
        scratch_shapes=scratch,
        compiler_params=pltpu.CompilerParams(
            dimension_semantics=("arbitrary",), vmem_limit_bytes=VMEM_LIMIT_BYTES),
        name=name,
    )(x, x, tail_a, tail_q, ssm, *params)


META_PAD_LEN = 128
PROMPT_TILE = 256
SAMPLE_PAD_LEN = 8
SAMPLE_SEQS_PER_STEP = 16


def kernel(x_prompt, x_sample, state_conv_a, state_conv_qkv, state_ssm, meta_tokens, norm_mix, w_in, conv_a_w,
           conv_qkv_w, a_log, dt_bias, gdn_norm, w_out, norm_ffn, w_up, w_down, norm_final):
    assert norm_mix.shape[0] == 1, "single layer"
    bp, tp, _ = x_prompt.shape
    bs, ts, _ = x_sample.shape
    n_meta = meta_tokens.shape[0]

    def row(v):
        return v.reshape(1, -1).astype(F32)

    def lane_row(v):
        return jnp.pad(v.astype(F32), (0, LANES - v.shape[0])).reshape(1, LANES)

    params = (
        row(norm_mix[0]),
        _bf(jnp.pad(w_in[0], ((0, 0), (0, IN_W_PAD - w_in.shape[2])))),
        conv_a_w[0].astype(F32), conv_qkv_w[0].astype(F32),
        lane_row(a_log[0]), lane_row(dt_bias[0]), row(gdn_norm[0]),
        _bf(w_out[0]), row(norm_ffn[0]), _bf(w_up[0]), _bf(w_down[0]), row(norm_final),
    )

    meta_cfg = Cfg(batch=1, nb=1, seq_len=META_PAD_LEN, valid_len=n_meta, chunk=64, n_tt=1, shared_state=False,
                   pipelined=False)
    meta_x = jnp.pad(meta_tokens.astype(F32), ((0, META_PAD_LEN - n_meta), (0, 0)))[None]
    _, ta_m, tq_m, ssm_m = _layer_call(
        meta_cfg, "layer_meta", meta_x,
        jnp.zeros((1, SCONV_K - 1, CONV_CH), F32), jnp.zeros((1, GDN_CONV_K - 1, QKV_W), F32),
        jnp.zeros((1, GDN_HEADS, GDN_DK, GDN_DV), F32), params)

    prompt_cfg = Cfg(batch=bp, nb=1, seq_len=PROMPT_TILE, valid_len=PROMPT_TILE, chunk=64,
                     n_tt=tp // PROMPT_TILE, shared_state=True, pipelined=True)
    y_p, ta_p, tq_p, ssm_p = _layer_call(prompt_cfg, "layer_prompt", x_prompt, ta_m, tq_m, ssm_m, params)

    sample_cfg = Cfg(batch=bs, nb=SAMPLE_SEQS_PER_STEP, seq_len=SAMPLE_PAD_LEN, valid_len=ts, chunk=SAMPLE_PAD_LEN,
                     n_tt=1, shared_state=False, pipelined=False)
    xs = jnp.pad(x_sample, ((0, 0), (0, SAMPLE_PAD_LEN - ts), (0, 0)))
    y_s, ta_s, tq_s, ssm_s = _layer_call(
        sample_cfg, "layer_sample", xs, state_conv_a[0], state_conv_qkv[0], state_ssm[0], params)

    return (y_p, y_s[:, :ts], ta_p[None], tq_p[None], ssm_p[None], ta_s[None], tq_s[None], ssm_s[None])
```

```python
import dataclasses
import functools

import jax
import jax.numpy as jnp
from jax import lax
from jax.experimental import pallas as pl
from jax.experimental.pallas import tpu as pltpu

F32 = jnp.float32
BF16 = jnp.bfloat16

D_MODEL = 1024
CONV_CH = 512
SCONV_K = 3
GDN_HEADS = 4
GDN_DK = 128
GDN_DV = 128
GDN_CONV_K = 4
QKV_W = GDN_HEADS * (2 * GDN_DK + GDN_DV)
FFN_HIDDEN = 4 * D_MODEL
EPS = 1e-6

O_B = 0
O_C = O_B + CONV_CH
O_H = O_C + CONV_CH
O_QKV = O_H + CONV_CH
O_G = O_QKV + QKV_W
O_AB = O_G + GDN_HEADS * GDN_DV
LANES = 128
SUBLANES = 8
IN_W_PAD = O_AB + LANES
VC_ROWS = 128
NEUMANN_BASE = 16
FFN_CHUNK = 256
VMEM_LIMIT_BYTES = 56 * 1024 * 1024

Q_COL = 0
K_COL = GDN_HEADS * GDN_DK
V_COL = 2 * GDN_HEADS * GDN_DK


@dataclasses.dataclass(frozen=True)
class Cfg:
    batch: int
    nb: int
    seq_len: int
    valid_len: int
    chunk: int
    n_tt: int
    shared_state: bool
    pipelined: bool

    @property
    def rows(self):
        return self.nb * self.seq_len

    @property
    def n_tiles(self):
        return (self.batch // self.nb) * self.n_tt

    @property
    def carry(self):
        return self.nb == 1


def _bf(x):
    return x.astype(BF16)


def _dot(a, b):
    return jnp.dot(a, b, preferred_element_type=F32)


def _dot_nt(a, b):
    return lax.dot_general(a, b, (((1,), (1,)), ((), ())), preferred_element_type=F32)


def _rms(x, g_row):
    return x * lax.rsqrt(jnp.mean(x * x, axis=-1, keepdims=True) + EPS) * g_row


def _softplus(x):
    return jnp.maximum(x, 0.0) + jnp.log1p(jnp.exp(-jnp.abs(x)))


def _silu(x):
    return x * jax.nn.sigmoid(x)


def _log2(n):
    k = n.bit_length() - 1
    assert (1 << k) == n, n
    return k


def _halves(x):
    return x[:, :LANES], x[:, LANES:]


def _pair(a, b):
    return jnp.concatenate([a, b], axis=1)


def _block_diag(y):
    ya, yb = _halves(y)
    z = jnp.zeros_like(ya)
    return jnp.concatenate([_pair(ya, z), _pair(z, yb)], axis=0)


def _pmm(x, y):
    return _dot(_bf(x), _block_diag(_bf(y)))


def _each(f, *lists):
    return [f(*xs) for xs in zip(*lists)]


def _problems(cfg):
    c = cfg.chunk
    if cfg.carry:
        assert 2 * c == VC_ROWS and GDN_HEADS == 4
        n = -(-cfg.valid_len // c)
        return [([(u * c, c, 0), (u * c, c, 2)], [(u * c, c, 1), (u * c, c, 3)]) for u in range(n)]
    half = VC_ROWS // GDN_HEADS
    assert half % cfg.seq_len == 0 and cfg.seq_len == c and cfg.rows % (2 * half) == 0
    return [([(u * 2 * half, half, h) for h in range(GDN_HEADS)],
             [(u * 2 * half + half, half, h) for h in range(GDN_HEADS)]) for u in range(cfg.rows // (2 * half))]


@dataclasses.dataclass
class Prep:
    u: jax.Array
    w: jax.Array
    qk: jax.Array
    q_dec: jax.Array
    k_dec: jax.Array
    g_last: tuple


def _delta_prep(cfg, problems, qkv_scr, gb_scr, gc_scr, out):
    c = cfg.chunk
    lc = _log2(c)
    ii = lax.broadcasted_iota(jnp.int32, (VC_ROWS, 2 * LANES), 0)
    jj = lax.broadcasted_iota(jnp.int32, (VC_ROWS, 2 * LANES), 1) & (LANES - 1)
    same = (ii >> lc) == (jj >> lc)
    lower = same & (jj <= ii)
    strict = same & (jj < ii)
    end_col = (jj == (ii | (c - 1)))[:, :LANES]
    eye = jnp.where(ii == jj, 1.0, 0.0)

    def stacked(ref, col, width, half):
        return jnp.concatenate([ref[r0:r0 + n, col(h):col(h) + width] for (r0, n, h) in half], axis=0)

    def packed(ref, base):
        return [_pair(*[stacked(ref, lambda h: base + LANES * h, LANES, half) for half in pr]) for pr in problems]

    def columns(ref, base):
        return [[stacked(ref, lambda h: base + h, 1, half) for half in pr] for pr in problems]

    def spread(cols):
        return _pair(*[jnp.broadcast_to(col, (VC_ROWS, LANES)) for col in cols])

    q, k, v = packed(qkv_scr, Q_COL), packed(qkv_scr, K_COL), packed(qkv_scr, V_COL)
    beta_c, gcum_c = columns(gb_scr, GDN_HEADS), columns(gc_scr, 0)
    beta, gcum = _each(spread, beta_c), _each(spread, gcum_c)
    g_rows = _each(lambda g: _pair(*[h.T for h in _halves(g)]), gcum)
    decay = _each(lambda g, gr: jnp.where(lower, jnp.exp(g - gr), 0.0), gcum, g_rows)
    g_last = _each(lambda gr: tuple(jnp.sum(jnp.where(end_col, h, 0.0), axis=1, keepdims=True) for h in _halves(gr)),
                   g_rows)
    kq = _each(lambda kk, qq: _dot_nt(_bf(jnp.concatenate([kk, qq], axis=0)), _block_diag(_bf(kk))), k, q)
    qk = _each(lambda m, d: m[VC_ROWS:] * d, kq, decay)
    a = _each(lambda m, b, d: jnp.where(strict, m[:VC_ROWS] * b * d, 0.0), kq, beta, decay)
    yield

    base = min(c, NEUMANN_BASE)
    lb = _log2(base)
    p = _each(lambda x: jnp.where((ii >> lb) == (jj >> lb), x, 0.0), a)
    t = _each(lambda x: eye - x, p)
    for _ in range(lb - 1):
        p = _each(_pmm, p, p)
        t = _each(lambda x, y: x + _pmm(x, y), t, p)
        yield
    size = base
    while size < c:
        ls = _log2(size)
        couple = ((ii >> (ls + 1)) == (jj >> (ls + 1))) & ((ii >> ls) != (jj >> ls))
        off = _each(lambda x: jnp.where(couple, x, 0.0), a)
        t_off = _each(_pmm, t, off)
        t = _each(lambda x, y: x - _pmm(y, x), t, t_off)
        size *= 2
        yield

    exp_g = _each(lambda cols: spread([jnp.exp(col) for col in cols]), gcum_c)
    k_scale = _each(lambda cols, gl: spread([jnp.exp(e - col) for col, e in zip(cols, gl)]), gcum_c, g_last)
    u = _each(lambda x, vv, b: _pmm(x, vv * b), t, v, beta)
    w = _each(lambda x, kk, b, e: _pmm(x, kk * (b * e)), t, k, beta, exp_g)
    q_dec = _each(lambda x, e: x * e, q, exp_g)
    k_dec = _each(lambda x, s: x * s, k, k_scale)
    out["preps"] = [Prep(*xs) for xs in zip(u, w, qk, q_dec, k_dec, g_last)]


def _delta_apply_carry(cfg, problems, preps, s_scr, o_scr):
    c = cfg.chunk
    top_rows = lax.broadcasted_iota(jnp.int32, (VC_ROWS, GDN_DV), 0) < c
    for (half_a, half_b), pr in zip(problems, preps):
        state = [s_scr[h] for h in range(GDN_HEADS)]
        w_s, q_s = [], []
        for blk in range(VC_ROWS // c):
            lo = blk * c
            wq = _bf(jnp.concatenate([pr.w[lo:lo + c], pr.q_dec[lo:lo + c]], axis=0))
            r = _dot(wq, _block_diag(_bf(_pair(state[half_a[blk][2]], state[half_b[blk][2]]))))
            w_s.append(r[:c])
            q_s.append(r[c:])
        v_new = pr.u - jnp.concatenate(w_s, axis=0)
        o = jnp.concatenate(q_s, axis=0) + _pmm(pr.qk, v_new)
        for half, vn, kd, gl, o_h in zip((half_a, half_b), _halves(v_new), _halves(pr.k_dec), pr.g_last, _halves(o)):
            upd = _dot(_bf(kd.T), _bf(_pair(jnp.where(top_rows, vn, 0.0), jnp.where(top_rows, 0.0, vn))))
            for blk, (r0, n, h) in enumerate(half):
                lo = blk * c
                s_scr[h] = state[h] * jnp.exp(gl[lo:lo + 1, :]) + upd[:, GDN_DV * blk:GDN_DV * (blk + 1)]
                o_scr[r0:r0 + n, GDN_DV * h:GDN_DV * (h + 1)] = o_h[lo:lo + c]


def _delta_apply_blocks(cfg, problems, preps, ssm_ref, nssm_ref, o_scr):
    c = cfg.chunk
    lc = _log2(c)
    blk_id = lax.broadcasted_iota(jnp.int32, (VC_ROWS, GDN_DV), 0) >> lc
    for pieces, pr in zip(problems, preps):
        halves = []
        for half, w_h, qd_h, u_h in zip(pieces, _halves(pr.w), _halves(pr.q_dec), _halves(pr.u)):
            blocks = [(r0 + t0, h, (r0 + t0) // cfg.seq_len) for (r0, n, h) in half for t0 in range(0, n, c)]
            vs, qs = [], []
            for b, (_, h, seq) in enumerate(blocks):
                lo = b * c
                r = _dot(_bf(jnp.concatenate([w_h[lo:lo + c], qd_h[lo:lo + c]], axis=0)), _bf(ssm_ref[seq, h]))
                vs.append(u_h[lo:lo + c] - r[:c])
                qs.append(r[c:])
            halves.append((blocks, jnp.concatenate(vs, axis=0), jnp.concatenate(qs, axis=0)))
        v_new = _pair(halves[0][1], halves[1][1])
        o = _pair(halves[0][2], halves[1][2]) + _pmm(pr.qk, v_new)
        for (blocks, vn, _), kd, gl, o_h in zip(halves, _halves(pr.k_dec), pr.g_last, _halves(o)):
            kd_t = _bf(kd.T)
            for b0 in range(0, len(blocks), 2):
                upd = _dot(kd_t, _bf(_pair(jnp.where(blk_id == b0, vn, 0.0), jnp.where(blk_id == b0 + 1, vn, 0.0))))
                for b in (b0, b0 + 1):
                    row, h, seq = blocks[b]
                    lo = b * c
                    nssm_ref[seq, h] = (ssm_ref[seq, h] * jnp.exp(gl[lo:lo + 1, :])
                                        + upd[:, GDN_DV * (b - b0):GDN_DV * (b - b0 + 1)])
                    o_scr[row:row + c, GDN_DV * h:GDN_DV * (h + 1)] = o_h[lo:lo + c]


def _layer_kernel(cfg, x_ref, xb_ref, ta_ref, tq_ref, ssm_ref, nmix_ref, win_ref, caw_ref, cqw_ref, alog_ref, dt_ref,
                  gnorm_ref, wout_ref, nffn_ref, wup_ref, wdown_ref, nfin_ref,
                  y_ref, nta_ref, ntq_ref, nssm_ref,
                  z_scr, xa_buf, xq_buf, qkv_scr, gb_scr, gc_scr, o_scr, s_scr, mix_scr):
    step = pl.program_id(0)
    nb, sl, rows = cfg.nb, cfg.seq_len, cfg.rows
    if cfg.pipelined:
        tile = jnp.minimum(step, cfg.n_tiles - 1)
        real = step < cfg.n_tiles
    else:
        tile = step
        real = True
    j = lax.rem(tile, cfg.n_tt)
    first = j == 0
    last = (j == cfg.n_tt - 1) & real

    @pl.when(first)
    def _():
        xa_buf[...] = jnp.zeros(xa_buf.shape, F32)
        xq_buf[...] = jnp.zeros(xq_buf.shape, F32)
        xa_buf[:, SUBLANES - (SCONV_K - 1):, :] = ta_ref[...]
        xq_buf[:, SUBLANES - (GDN_CONV_K - 1):, :] = tq_ref[...]
        if cfg.carry:
            s_scr[...] = ssm_ref[0]

    if cfg.pipelined:
        @pl.when(step == 0)
        def _():
            mix_scr[...] = jnp.zeros(mix_scr.shape, mix_scr.dtype)

    st = {}
    problems = _problems(cfg)

    def proj(lo, hi):
        z_scr[:, lo:hi] = _dot(st["xn"], win_ref[:, lo:hi])

    tail_lo = (cfg.valid_len - 1) // SUBLANES * SUBLANES
    assert cfg.valid_len - tail_lo >= GDN_CONV_K - 1 and (cfg.n_tt == 1 or cfg.valid_len == sl)

    def causal_conv(x, prev_buf, w_ref, col, width):
        taps = w_ref.shape[0]
        t8 = lax.broadcasted_iota(jnp.int32, (nb, SUBLANES, width), 1)
        prev = prev_buf[:, :, col:col + width]
        acc = x * w_ref[taps - 1:taps, col:col + width]
        for back in range(1, taps):
            moved = pltpu.roll(x, back, 1)
            head = jnp.where(t8 < back, pltpu.roll(prev, back, 1), moved[:, :SUBLANES, :])
            moved = head if sl == SUBLANES else jnp.concatenate([head, moved[:, SUBLANES:, :]], axis=1)
            acc = acc + moved * w_ref[taps - 1 - back:taps - back, col:col + width]
        prev_buf[:, :, col:col + width] = x[:, tail_lo:tail_lo + SUBLANES, :]
        return acc

    def conv_q(col):
        width = GDN_HEADS * GDN_DK
        x = z_scr[:, O_QKV + col:O_QKV + col + width].reshape(nb, sl, width)
        act = _silu(causal_conv(x, xq_buf, cqw_ref, col, width).reshape(rows, width))
        if col == V_COL:
            qkv_scr[:, col:col + width] = act
            return
        scale = GDN_DK ** -0.5 if col == Q_COL else 1.0
        for h in range(GDN_HEADS):
            ah = act[:, GDN_DK * h:GDN_DK * (h + 1)]
            qkv_scr[:, col + GDN_DK * h:col + GDN_DK * (h + 1)] = (
                ah * (lax.rsqrt(jnp.sum(ah * ah, axis=-1, keepdims=True) + EPS) * scale))

    def conv_a():
        x = (z_scr[:, O_C:O_H] * z_scr[:, O_H:O_QKV]).reshape(nb, sl, CONV_CH)
        acc = causal_conv(x, xa_buf, caw_ref, 0, CONV_CH)
        mix_scr[:, :CONV_CH] = _bf(z_scr[:, O_B:O_C] * acc.reshape(rows, CONV_CH))

    def gates():
        zab = z_scr[:, O_AB:IN_W_PAD]
        g = -jnp.exp(alog_ref[...]) * _softplus(zab + dt_ref[...])
        beta = jax.nn.sigmoid(zab)
        lane = lax.broadcasted_iota(jnp.int32, (rows, LANES), 1)
        gb = jnp.where(lane < GDN_HEADS, g, beta)
        if cfg.valid_len < sl:
            t_in_seq = lax.broadcasted_iota(jnp.int32, (rows, LANES), 0) & (sl - 1)
            gb = jnp.where(t_in_seq < cfg.valid_len, gb, 0.0)
        gb_scr[...] = gb
        ri = lax.broadcasted_iota(jnp.int32, (rows, rows), 0)
        rj = lax.broadcasted_iota(jnp.int32, (rows, rows), 1)
        lc = _log2(cfg.chunk)
        tri = jnp.where(((ri >> lc) == (rj >> lc)) & (rj <= ri), 1.0, 0.0).astype(BF16)
        g1 = _bf(gb)
        r1 = gb - g1.astype(F32)
        g2 = _bf(r1)
        g3 = _bf(r1 - g2.astype(F32))
        gc_scr[...] = _dot(tri, g1) + (_dot(tri, g2) + _dot(tri, g3))

    def gated_norm():
        for h in range(GDN_HEADS):
            oh = o_scr[:, GDN_DV * h:GDN_DV * (h + 1)]
            gate = z_scr[:, O_G + GDN_DV * h:O_G + GDN_DV * (h + 1)]
            mix_scr[:, CONV_CH + GDN_DV * h:CONV_CH + GDN_DV * (h + 1)] = _bf(_rms(oh, gnorm_ref[...]) * _silu(gate))

    def front_half():
        st["xn"] = _bf(_rms(x_ref[...].reshape(rows, D_MODEL), nmix_ref[...]))
        proj(O_QKV, O_G)
        yield 0
        proj(O_B, O_QKV)
        yield 0
        conv_q(Q_COL)
        yield 0
        proj(O_G, IN_W_PAD)
        yield 0
        conv_q(K_COL)
        yield 3
        conv_q(V_COL)
        yield 2
        conv_a()
        yield 3
        gates()
        yield 2
        for i, _ in enumerate(_delta_prep(cfg, problems, qkv_scr, gb_scr, gc_scr, st)):
            yield 2 if i == 0 else 1
        done = len(problems) * cfg.chunk
        if cfg.carry and done < sl:
            o_scr[done:, :] = jnp.zeros((sl - done, GDN_HEADS * GDN_DV), F32)
        yield 2
        for i in range(len(problems)):
            if cfg.carry:
                _delta_apply_carry(cfg, problems[i:i + 1], st["preps"][i:i + 1], s_scr, o_scr)
            else:
                _delta_apply_blocks(cfg, problems[i:i + 1], st["preps"][i:i + 1], ssm_ref, nssm_ref, o_scr)
            yield 3
        gated_norm()
        yield 1

    def back_half():
        h1 = xb_ref[...].reshape(rows, D_MODEL) + _dot(mix_scr[...], wout_ref[...])
        hn = _bf(_rms(h1, nffn_ref[...]))
        acc = h1
        yield
        for c in range(FFN_HIDDEN // FFN_CHUNK):
            up = jnp.maximum(_dot(hn, wup_ref[c]), 0.0)
            yield
            acc = acc + _dot(_bf(up * up), wdown_ref[c])
            yield
        y_ref[...] = _rms(acc, nfin_ref[...]).reshape(nb, sl, D_MODEL)

    back = back_half()
    if cfg.pipelined:
        next(back)
        for n_back in front_half():
            for _ in range(n_back):
                next(back, None)
    else:
        for _ in front_half():
            pass
    for _ in back:
        pass

    @pl.when(last)
    def _():
        end = cfg.valid_len - tail_lo
        nta_ref[...] = xa_buf[:, end - (SCONV_K - 1):end, :]
        ntq_ref[...] = xq_buf[:, end - (GDN_CONV_K - 1):end, :]
        if cfg.carry:
            nssm_ref[0] = s_scr[...]


def _layer_call(cfg, name, x, tail_a, tail_q, ssm, params):
    nb, sl, rows = cfg.nb, cfg.seq_len, cfg.rows
    assert x.shape == (cfg.batch, sl * cfg.n_tt, D_MODEL), (x.shape, cfg)
    assert cfg.batch % nb == 0 and sl % SUBLANES == 0 and rows % VC_ROWS == 0
    assert not cfg.shared_state or nb == 1
    n_tiles, n_tt = cfg.n_tiles, cfg.n_tt

    def front_tile(s):
        return jnp.minimum(s, n_tiles - 1) if cfg.pipelined else s

    def back_tile(s):
        return jnp.maximum(s - 1, 0) if cfg.pipelined else s

    def tile_spec(which):
        return pl.BlockSpec((nb, sl, D_MODEL), lambda s: (which(s) // n_tt, which(s) % n_tt, 0))

    def state_spec(shape):
        zeros = (0,) * len(shape)
        if cfg.shared_state:
            return pl.BlockSpec((nb,) + shape, lambda s: (0,) + zeros)
        return pl.BlockSpec((nb,) + shape, lambda s: (front_tile(s) // n_tt,) + zeros)

    def out_state_spec(shape):
        zeros = (0,) * len(shape)
        return pl.BlockSpec((nb,) + shape, lambda s: (front_tile(s) // n_tt,) + zeros)

    def const_spec(arr):
        zeros = (0,) * arr.ndim
        return pl.BlockSpec(arr.shape, lambda s: zeros, pipeline_mode=pl.Buffered(1))

    ta_shape, tq_shape = (SCONV_K - 1, CONV_CH), (GDN_CONV_K - 1, QKV_W)
    ssm_shape = (GDN_HEADS, GDN_DK, GDN_DV)
    in_specs = [tile_spec(front_tile), tile_spec(back_tile),
                state_spec(ta_shape), state_spec(tq_shape), state_spec(ssm_shape)]
    in_specs += [const_spec(p) for p in params]
    out_specs = [tile_spec(back_tile), out_state_spec(ta_shape), out_state_spec(tq_shape), out_state_spec(ssm_shape)]
    out_shape = [
        jax.ShapeDtypeStruct(x.shape, F32),
        jax.ShapeDtypeStruct((cfg.batch,) + ta_shape, F32),
        jax.ShapeDtypeStruct((cfg.batch,) + tq_shape, F32),
        jax.ShapeDtypeStruct((cfg.batch,) + ssm_shape, F32),
    ]
    scratch = [
        pltpu.VMEM((rows, IN_W_PAD), F32),
        pltpu.VMEM((nb, SUBLANES, CONV_CH), F32),
        pltpu.VMEM((nb, SUBLANES, QKV_W), F32),
        pltpu.VMEM((rows, QKV_W), F32),
        pltpu.VMEM((rows, LANES), F32),
        pltpu.VMEM((rows, LANES), F32),
        pltpu.VMEM((rows, GDN_HEADS * GDN_DV), F32),
        pltpu.VMEM(ssm_shape, F32),
        pltpu.VMEM((rows, D_MODEL), BF16),
    ]
    return pl.pallas_call(
        functools.partial(_layer_kernel, cfg),
        grid=(n_tiles + (1 if cfg.pipelined else 0),),
        in_specs=in_specs,
        out_specs=out_specs,
        out_shape=out_shape,
        scratch_shapes=scratch,
        compiler_params=pltpu.CompilerParams(
            dimension_semantics=("arbitrary",), vmem_limit_bytes=VMEM_LIMIT_BYTES),
        name=name,
    )(x, x, tail_a, tail_q, ssm, *params)


META_PAD_LEN = 128
PROMPT_TILE = 256
SAMPLE_PAD_LEN = 8
SAMPLE_SEQS_PER_STEP = 16


def kernel(x_prompt, x_sample, state_conv_a, state_conv_qkv, state_ssm, meta_tokens, norm_mix, w_in, conv_a_w,
           conv_qkv_w, a_log, dt_bias, gdn_norm, w_out, norm_ffn, w_up, w_down, norm_final):
    assert norm_mix.shape[0] == 1, "single layer"
    bp, tp, _ = x_prompt.shape
    bs, ts, _ = x_sample.shape
    n_meta = meta_tokens.shape[0]

    def row(v):
        return v.reshape(1, -1).astype(F32)

    def lane_row(v):
        return jnp.pad(v.astype(F32), (0, LANES - v.shape[0])).reshape(1, LANES)

    params = (
        row(norm_mix[0]),
        _bf(jnp.pad(w_in[0], ((0, 0), (0, IN_W_PAD - w_in.shape[2])))),
        conv_a_w[0].astype(F32), conv_qkv_w[0].astype(F32),
        lane_row(a_log[0]), lane_row(dt_bias[0]), row(gdn_norm[0]),
        _bf(w_out[0]), row(norm_ffn[0]),
        _bf(w_up[0]).reshape(D_MODEL, FFN_HIDDEN // FFN_CHUNK, FFN_CHUNK).transpose(1, 0, 2),
        _bf(w_down[0]).reshape(FFN_HIDDEN // FFN_CHUNK, FFN_CHUNK, D_MODEL),
        row(norm_final),
    )

    meta_cfg = Cfg(batch=1, nb=1, seq_len=META_PAD_LEN, valid_len=n_meta, chunk=64, n_tt=1, shared_state=False,
                   pipelined=False)
    meta_x = jnp.pad(meta_tokens.astype(F32), ((0, META_PAD_LEN - n_meta), (0, 0)))[None]
    _, ta_m, tq_m, ssm_m = _layer_call(
        meta_cfg, "layer_meta", meta_x,
        jnp.zeros((1, SCONV_K - 1, CONV_CH), F32), jnp.zeros((1, GDN_CONV_K - 1, QKV_W), F32),
        jnp.zeros((1, GDN_HEADS, GDN_DK, GDN_DV), F32), params)

    prompt_cfg = Cfg(batch=bp, nb=1, seq_len=PROMPT_TILE, valid_len=PROMPT_TILE, chunk=64,
                     n_tt=tp // PROMPT_TILE, shared_state=True, pipelined=True)
    y_p, ta_p, tq_p, ssm_p = _layer_call(prompt_cfg, "layer_prompt", x_prompt, ta_m, tq_m, ssm_m, params)

    sample_cfg = Cfg(batch=bs, nb=SAMPLE_SEQS_PER_STEP, seq_len=SAMPLE_PAD_LEN, valid_len=ts, chunk=SAMPLE_PAD_LEN,
                     n_tt=1, shared_state=False, pipelined=False)
    xs = jnp.pad(x_sample, ((0, 0), (0, SAMPLE_PAD_LEN - ts), (0, 0)))
    y_s, ta_s, tq_s, ssm_s = _layer_call(
        sample_cfg, "layer_sample", xs, state_conv_a[0], state_conv_qkv[0], state_ssm[0], params)

    return (y_p, y_s[:, :ts], ta_p[None], tq_p[None], ssm_p[None], ta_s[None], tq_s[None], ssm_s[None])
```

```python
import dataclasses
import functools

import jax
import jax.numpy as jnp
from jax import lax
from jax.experimental import pallas as pl
from jax.experimental.pallas import tpu as pltpu

F32 = jnp.float32
BF16 = jnp.bfloat16

D_MODEL = 1024
CONV_CH = 512
SCONV_K = 3
GDN_HEADS = 4
GDN_DK = 128
GDN_DV = 128
GDN_CONV_K = 4
QKV_W = GDN_HEADS * (2 * GDN_DK + GDN_DV)
FFN_HIDDEN = 4 * D_MODEL
EPS = 1e-6

O_B = 0
O_C = O_B + CONV_CH
O_H = O_C + CONV_CH
O_QKV = O_H + CONV_CH
O_G = O_QKV + QKV_W
O_AB = O_G + GDN_HEADS * GDN_DV
LANES = 128
SUBLANES = 8
IN_W_PAD = O_AB + LANES
VC_ROWS = 128
NEUMANN_BASE = 16
FFN_CHUNK = 256
VMEM_LIMIT_BYTES = 56 * 1024 * 1024

Q_COL = 0
K_COL = GDN_HEADS * GDN_DK
V_COL = 2 * GDN_HEADS * GDN_DK


@dataclasses.dataclass(frozen=True)
class Cfg:
    batch: int
    nb: int
    seq_len: int
    valid_len: int
    chunk: int
    n_tt: int
    shared_state: bool
    pipelined: bool

    @property
    def rows(self):
        return self.nb * self.seq_len

    @property
    def n_tiles(self):
        return (self.batch // self.nb) * self.n_tt

    @property
    def x_len(self):
        return self.valid_len if self.n_tt == 1 else self.seq_len

    @property
    def carry(self):
        return self.nb == 1


def _bf(x):
    return x.astype(BF16)


def _dot(a, b):
    return jnp.dot(a, b, preferred_element_type=F32)


def _dot_nt(a, b):
    return lax.dot_general(a, b, (((1,), (1,)), ((), ())), preferred_element_type=F32)


def _rms(x, g_row):
    return x * lax.rsqrt(jnp.mean(x * x, axis=-1, keepdims=True) + EPS) * g_row


def _softplus(x):
    return jnp.maximum(x, 0.0) + jnp.log1p(jnp.exp(-jnp.abs(x)))


def _silu(x):
    return x * jax.nn.sigmoid(x)


def _log2(n):
    k = n.bit_length() - 1
    assert (1 << k) == n, n
    return k


def _halves(x):
    return x[:, :LANES], x[:, LANES:]


def _pair(a, b):
    return jnp.concatenate([a, b], axis=1)


def _block_diag(y):
    ya, yb = _halves(y)
    z = jnp.zeros_like(ya)
    return jnp.concatenate([_pair(ya, z), _pair(z, yb)], axis=0)


def _pmm(x, y):
    return _dot(_bf(x), _block_diag(_bf(y)))


def _each(f, *lists):
    return [f(*xs) for xs in zip(*lists)]


def _problems(cfg):
    c = cfg.chunk
    if cfg.carry:
        assert 2 * c == VC_ROWS and GDN_HEADS == 4
        n = -(-cfg.valid_len // c)
        return [([(u * c, c, 0), (u * c, c, 2)], [(u * c, c, 1), (u * c, c, 3)]) for u in range(n)]
    half = VC_ROWS // GDN_HEADS
    assert half % cfg.seq_len == 0 and cfg.seq_len == c and cfg.rows % (2 * half) == 0
    return [([(u * 2 * half, half, h) for h in range(GDN_HEADS)],
             [(u * 2 * half + half, half, h) for h in range(GDN_HEADS)]) for u in range(cfg.rows // (2 * half))]


@dataclasses.dataclass
class Prep:
    u: jax.Array
    w: jax.Array
    qk: jax.Array
    q_dec: jax.Array
    k_dec: jax.Array
    g_last: tuple


def _delta_prep(cfg, problems, qkv_scr, gb_scr, gc_scr, out):
    c = cfg.chunk
    lc = _log2(c)
    ii = lax.broadcasted_iota(jnp.int32, (VC_ROWS, 2 * LANES), 0)
    jj = lax.broadcasted_iota(jnp.int32, (VC_ROWS, 2 * LANES), 1) & (LANES - 1)
    same = (ii >> lc) == (jj >> lc)
    lower = same & (jj <= ii)
    strict = same & (jj < ii)
    end_col = (jj == (ii | (c - 1)))[:, :LANES]
    eye = jnp.where(ii == jj, 1.0, 0.0)

    def stacked(ref, col, width, half):
        return jnp.concatenate([ref[r0:r0 + n, col(h):col(h) + width] for (r0, n, h) in half], axis=0)

    def packed(ref, base):
        return [_pair(*[stacked(ref, lambda h: base + LANES * h, LANES, half) for half in pr]) for pr in problems]

    def columns(ref, base):
        return [[stacked(ref, lambda h: base + h, 1, half) for half in pr] for pr in problems]

    def spread(cols):
        return _pair(*[jnp.broadcast_to(col, (VC_ROWS, LANES)) for col in cols])

    q, k, v = packed(qkv_scr, Q_COL), packed(qkv_scr, K_COL), packed(qkv_scr, V_COL)
    beta_c, gcum_c = columns(gb_scr, GDN_HEADS), columns(gc_scr, 0)
    beta, gcum = _each(spread, beta_c), _each(spread, gcum_c)
    g_rows = _each(lambda g: _pair(*[h.T for h in _halves(g)]), gcum)
    decay = _each(lambda g, gr: jnp.where(lower, jnp.exp(g - gr), 0.0), gcum, g_rows)
    g_last = _each(lambda gr: tuple(jnp.sum(jnp.where(end_col, h, 0.0), axis=1, keepdims=True) for h in _halves(gr)),
                   g_rows)
    kq = _each(lambda kk, qq: _dot_nt(_bf(jnp.concatenate([kk, qq], axis=0)), _block_diag(_bf(kk))), k, q)
    qk = _each(lambda m, d: m[VC_ROWS:] * d, kq, decay)
    a = _each(lambda m, b, d: jnp.where(strict, m[:VC_ROWS] * b * d, 0.0), kq, beta, decay)
    yield

    base = min(c, NEUMANN_BASE)
    lb = _log2(base)
    p = _each(lambda x: jnp.where((ii >> lb) == (jj >> lb), x, 0.0), a)
    t = _each(lambda x: eye - x, p)
    for _ in range(lb - 1):
        p = _each(_pmm, p, p)
        t = _each(lambda x, y: x + _pmm(x, y), t, p)
        yield
    size = base
    while size < c:
        ls = _log2(size)
        couple = ((ii >> (ls + 1)) == (jj >> (ls + 1))) & ((ii >> ls) != (jj >> ls))
        off = _each(lambda x: jnp.where(couple, x, 0.0), a)
        t_off = _each(_pmm, t, off)
        t = _each(lambda x, y: x - _pmm(y, x), t, t_off)
        size *= 2
        yield

    exp_g = _each(lambda cols: spread([jnp.exp(col) for col in cols]), gcum_c)
    k_scale = _each(lambda cols, gl: spread([jnp.exp(e - col) for col, e in zip(cols, gl)]), gcum_c, g_last)
    u = _each(lambda x, vv, b: _pmm(x, vv * b), t, v, beta)
    w = _each(lambda x, kk, b, e: _pmm(x, kk * (b * e)), t, k, beta, exp_g)
    q_dec = _each(lambda x, e: x * e, q, exp_g)
    k_dec = _each(lambda x, s: x * s, k, k_scale)
    out["preps"] = [Prep(*xs) for xs in zip(u, w, qk, q_dec, k_dec, g_last)]


def _delta_apply_carry(cfg, problems, preps, s_scr, o_scr):
    c = cfg.chunk
    top_rows = lax.broadcasted_iota(jnp.int32, (VC_ROWS, GDN_DV), 0) < c
    for (half_a, half_b), pr in zip(problems, preps):
        state = [s_scr[h] for h in range(GDN_HEADS)]
        w_s, q_s = [], []
        for blk in range(VC_ROWS // c):
            lo = blk * c
            wq = _bf(jnp.concatenate([pr.w[lo:lo + c], pr.q_dec[lo:lo + c]], axis=0))
            r = _dot(wq, _block_diag(_bf(_pair(state[half_a[blk][2]], state[half_b[blk][2]]))))
            w_s.append(r[:c])
            q_s.append(r[c:])
        v_new = pr.u - jnp.concatenate(w_s, axis=0)
        o = jnp.concatenate(q_s, axis=0) + _pmm(pr.qk, v_new)
        for half, vn, kd, gl, o_h in zip((half_a, half_b), _halves(v_new), _halves(pr.k_dec), pr.g_last, _halves(o)):
            upd = _dot(_bf(kd.T), _bf(_pair(jnp.where(top_rows, vn, 0.0), jnp.where(top_rows, 0.0, vn))))
            for blk, (r0, n, h) in enumerate(half):
                lo = blk * c
                s_scr[h] = state[h] * jnp.exp(gl[lo:lo + 1, :]) + upd[:, GDN_DV * blk:GDN_DV * (blk + 1)]
                o_scr[r0:r0 + n, GDN_DV * h:GDN_DV * (h + 1)] = o_h[lo:lo + c]


def _delta_apply_blocks(cfg, problems, preps, ssm_ref, nssm_ref, o_scr):
    c = cfg.chunk
    lc = _log2(c)
    blk_id = lax.broadcasted_iota(jnp.int32, (VC_ROWS, GDN_DV), 0) >> lc
    for pieces, pr in zip(problems, preps):
        halves = []
        for half, w_h, qd_h, u_h in zip(pieces, _halves(pr.w), _halves(pr.q_dec), _halves(pr.u)):
            blocks = [(r0 + t0, h, (r0 + t0) // cfg.seq_len) for (r0, n, h) in half for t0 in range(0, n, c)]
            vs, qs = [], []
            for b, (_, h, seq) in enumerate(blocks):
                lo = b * c
                r = _dot(_bf(jnp.concatenate([w_h[lo:lo + c], qd_h[lo:lo + c]], axis=0)), _bf(ssm_ref[seq, h]))
                vs.append(u_h[lo:lo + c] - r[:c])
                qs.append(r[c:])
            halves.append((blocks, jnp.concatenate(vs, axis=0), jnp.concatenate(qs, axis=0)))
        v_new = _pair(halves[0][1], halves[1][1])
        o = _pair(halves[0][2], halves[1][2]) + _pmm(pr.qk, v_new)
        for (blocks, vn, _), kd, gl, o_h in zip(halves, _halves(pr.k_dec), pr.g_last, _halves(o)):
            kd_t = _bf(kd.T)
            for b0 in range(0, len(blocks), 2):
                upd = _dot(kd_t, _bf(_pair(jnp.where(blk_id == b0, vn, 0.0), jnp.where(blk_id == b0 + 1, vn, 0.0))))
                for b in (b0, b0 + 1):
                    row, h, seq = blocks[b]
                    lo = b * c
                    nssm_ref[seq, h] = (ssm_ref[seq, h] * jnp.exp(gl[lo:lo + 1, :])
                                        + upd[:, GDN_DV * (b - b0):GDN_DV * (b - b0 + 1)])
                    o_scr[row:row + c, GDN_DV * h:GDN_DV * (h + 1)] = o_h[lo:lo + c]


def _layer_kernel(cfg, x_ref, xb_ref, ta_ref, tq_ref, ssm_ref, nmix_ref, win_ref, wab_ref, caw_ref, cqw_ref, alog_ref, dt_ref,
                  gnorm_ref, wout_ref, nffn_ref, wup_ref, wdown_ref, nfin_ref,
                  y_ref, nta_ref, ntq_ref, nssm_ref,
                  z_scr, xa_buf, xq_buf, qkv_scr, gb_scr, gc_scr, o_scr, s_scr, mix_scr):
    step = pl.program_id(0)
    nb, sl, rows = cfg.nb, cfg.seq_len, cfg.rows
    if cfg.pipelined:
        tile = jnp.minimum(step, cfg.n_tiles - 1)
        real = step < cfg.n_tiles
    else:
        tile = step
        real = True
    j = lax.rem(tile, cfg.n_tt)
    first = j == 0
    last = (j == cfg.n_tt - 1) & real

    @pl.when(first)
    def _():
        xa_buf[...] = jnp.zeros(xa_buf.shape, F32)
        xq_buf[...] = jnp.zeros(xq_buf.shape, F32)
        xa_buf[:, SUBLANES - (SCONV_K - 1):, :] = ta_ref[...]
        xq_buf[:, SUBLANES - (GDN_CONV_K - 1):, :] = tq_ref[...]
        if cfg.carry:
            s_scr[...] = ssm_ref[0]

    st = {}
    problems = _problems(cfg)

    def padded_rows(ref):
        x = ref[...]
        if cfg.x_len < sl:
            x = jnp.concatenate([x, jnp.zeros((nb, sl - cfg.x_len, D_MODEL), F32)], axis=1)
        return x.reshape(rows, D_MODEL)

    def proj(lo, hi):
        z_scr[:, lo:hi] = _dot(st["xn"], win_ref[:, lo:hi])

    if cfg.pipelined:
        @pl.when(step == 0)
        def _():
            mix_scr[...] = jnp.zeros(mix_scr.shape, mix_scr.dtype)

    tail_lo = (cfg.valid_len - 1) // SUBLANES * SUBLANES
    assert cfg.valid_len - tail_lo >= GDN_CONV_K - 1 and (cfg.n_tt == 1 or cfg.valid_len == sl)

    def causal_conv(x, prev_buf, w_ref, col, width):
        taps = w_ref.shape[0]
        t8 = lax.broadcasted_iota(jnp.int32, (nb, SUBLANES, width), 1)
        prev = prev_buf[:, :, col:col + width]
        acc = x * w_ref[taps - 1:taps, col:col + width]
        for back in range(1, taps):
            moved = pltpu.roll(x, back, 1)
            head = jnp.where(t8 < back, pltpu.roll(prev, back, 1), moved[:, :SUBLANES, :])
            moved = head if sl == SUBLANES else jnp.concatenate([head, moved[:, SUBLANES:, :]], axis=1)
            acc = acc + moved * w_ref[taps - 1 - back:taps - back, col:col + width]
        prev_buf[:, :, col:col + width] = x[:, tail_lo:tail_lo + SUBLANES, :]
        return acc

    def conv_q(col):
        width = GDN_HEADS * GDN_DK
        x = z_scr[:, O_QKV + col:O_QKV + col + width].reshape(nb, sl, width)
        act = _silu(causal_conv(x, xq_buf, cqw_ref, col, width).reshape(rows, width))
        if col == V_COL:
            qkv_scr[:, col:col + width] = act
            return
        scale = GDN_DK ** -0.5 if col == Q_COL else 1.0
        for h in range(GDN_HEADS):
            ah = act[:, GDN_DK * h:GDN_DK * (h + 1)]
            qkv_scr[:, col + GDN_DK * h:col + GDN_DK * (h + 1)] = (
                ah * (lax.rsqrt(jnp.sum(ah * ah, axis=-1, keepdims=True) + EPS) * scale))

    def conv_a():
        x = (z_scr[:, O_C:O_H] * z_scr[:, O_H:O_QKV]).reshape(nb, sl, CONV_CH)
        acc = causal_conv(x, xa_buf, caw_ref, 0, CONV_CH)
        mix_scr[:, :CONV_CH] = _bf(z_scr[:, O_B:O_C] * acc.reshape(rows, CONV_CH))

    def gates():
        zab = z_scr[:, O_AB:IN_W_PAD]
        g = -jnp.exp(alog_ref[...]) * _softplus(zab + dt_ref[...])
        beta = jax.nn.sigmoid(zab)
        lane = lax.broadcasted_iota(jnp.int32, (rows, LANES), 1)
        gb = jnp.where(lane < GDN_HEADS, g, beta)
        if cfg.valid_len < sl:
            t_in_seq = lax.broadcasted_iota(jnp.int32, (rows, LANES), 0) & (sl - 1)
            gb = jnp.where(t_in_seq < cfg.valid_len, gb, 0.0)
        gb_scr[...] = gb
        ri = lax.broadcasted_iota(jnp.int32, (rows, rows), 0)
        rj = lax.broadcasted_iota(jnp.int32, (rows, rows), 1)
        lc = _log2(cfg.chunk)
        tri = jnp.where(((ri >> lc) == (rj >> lc)) & (rj <= ri), 1.0, 0.0).astype(BF16)
        g1 = _bf(gb)
        r1 = gb - g1.astype(F32)
        g2 = _bf(r1)
        g3 = _bf(r1 - g2.astype(F32))
        gc_scr[...] = _dot(tri, g1) + (_dot(tri, g2) + _dot(tri, g3))

    def gated_norm():
        for h in range(GDN_HEADS):
            oh = o_scr[:, GDN_DV * h:GDN_DV * (h + 1)]
            gate = z_scr[:, O_G + GDN_DV * h:O_G + GDN_DV * (h + 1)]
            mix_scr[:, CONV_CH + GDN_DV * h:CONV_CH + GDN_DV * (h + 1)] = _bf(_rms(oh, gnorm_ref[...]) * _silu(gate))

    def project(src_ref, groups):
        if "xn" not in st:
            st["xn"] = _bf(_rms(padded_rows(src_ref), nmix_ref[...]))
        for group in groups:
            if group == 0:
                proj(O_QKV, O_G)
            elif group == 1:
                proj(O_B, O_QKV)
            else:
                proj(O_G, O_AB)
                z_scr[:, O_AB:] = _dot(st["xn"], wab_ref[...])

    def front_half():
        project(x_ref, (0, 1))
        yield 0
        conv_q(Q_COL)
        yield 0
        project(x_ref, (2,))
        yield 0
        conv_q(K_COL)
        yield 3
        conv_q(V_COL)
        yield 2
        conv_a()
        yield 3
        gates()
        yield 2
        for i, _ in enumerate(_delta_prep(cfg, problems, qkv_scr, gb_scr, gc_scr, st)):
            yield 2 if i == 0 else 1
        done = len(problems) * cfg.chunk
        if cfg.carry and done < sl:
            o_scr[done:, :] = jnp.zeros((sl - done, GDN_HEADS * GDN_DV), F32)
        yield 2
        for i in range(len(problems)):
            if cfg.carry:
                _delta_apply_carry(cfg, problems[i:i + 1], st["preps"][i:i + 1], s_scr, o_scr)
            else:
                _delta_apply_blocks(cfg, problems[i:i + 1], st["preps"][i:i + 1], ssm_ref, nssm_ref, o_scr)
            yield 3
        gated_norm()
        yield 1

    def back_half():
        h1 = padded_rows(xb_ref) + _dot(mix_scr[...], wout_ref[...])
        hn = _bf(_rms(h1, nffn_ref[...]))
        acc = h1
        yield
        for c in range(FFN_HIDDEN // FFN_CHUNK):
            up = jnp.maximum(_dot(hn, wup_ref[:, c * FFN_CHUNK:(c + 1) * FFN_CHUNK]), 0.0)
            yield
            acc = acc + _dot(_bf(up * up), wdown_ref[c])
            yield
        y_ref[...] = _rms(acc, nfin_ref[...]).reshape(nb, sl, D_MODEL)[:, :cfg.x_len, :]

    back = back_half()
    if cfg.pipelined:
        next(back)
        for n_back in front_half():
            for _ in range(n_back):
                next(back, None)
    else:
        for _ in front_half():
            pass
    for _ in back:
        pass

    @pl.when(last)
    def _():
        end = cfg.valid_len - tail_lo
        nta_ref[...] = xa_buf[:, end - (SCONV_K - 1):end, :]
        ntq_ref[...] = xq_buf[:, end - (GDN_CONV_K - 1):end, :]
        if cfg.carry:
            nssm_ref[0] = s_scr[...]


def _layer_call(cfg, name, x, tail_a, tail_q, ssm, params):
    nb, sl, rows = cfg.nb, cfg.seq_len, cfg.rows
    assert x.shape == (cfg.batch, cfg.x_len * cfg.n_tt, D_MODEL), (x.shape, cfg)
    assert cfg.batch % nb == 0 and sl % SUBLANES == 0 and rows % VC_ROWS == 0
    assert not cfg.shared_state or nb == 1
    n_tiles, n_tt = cfg.n_tiles, cfg.n_tt

    def front_tile(s):
        return jnp.minimum(s, n_tiles - 1) if cfg.pipelined else s

    def back_tile(s):
        return jnp.maximum(s - 1, 0) if cfg.pipelined else s

    def tile_spec(which):
        return pl.BlockSpec((nb, cfg.x_len, D_MODEL), lambda s: (which(s) // n_tt, which(s) % n_tt, 0))

    def state_spec(shape):
        zeros = (0,) * len(shape)
        if cfg.shared_state:
            return pl.BlockSpec((nb,) + shape, lambda s: (0,) + zeros)
        return pl.BlockSpec((nb,) + shape, lambda s: (front_tile(s) // n_tt,) + zeros)

    def out_state_spec(shape):
        zeros = (0,) * len(shape)
        return pl.BlockSpec((nb,) + shape, lambda s: (front_tile(s) // n_tt,) + zeros)

    def const_spec(arr):
        zeros = (0,) * arr.ndim
        return pl.BlockSpec(arr.shape, lambda s: zeros, pipeline_mode=pl.Buffered(1))

    ta_shape, tq_shape = (SCONV_K - 1, CONV_CH), (GDN_CONV_K - 1, QKV_W)
    ssm_shape = (GDN_HEADS, GDN_DK, GDN_DV)
    in_specs = [tile_spec(front_tile), tile_spec(back_tile),
                state_spec(ta_shape), state_spec(tq_shape), state_spec(ssm_shape)]
    in_specs += [const_spec(p) for p in params]
    out_specs = [tile_spec(back_tile), out_state_spec(ta_shape), out_state_spec(tq_shape), out_state_spec(ssm_shape)]
    out_shape = [
        jax.ShapeDtypeStruct(x.shape, F32),
        jax.ShapeDtypeStruct((cfg.batch,) + ta_shape, F32),
        jax.ShapeDtypeStruct((cfg.batch,) + tq_shape, F32),
        jax.ShapeDtypeStruct((cfg.batch,) + ssm_shape, F32),
    ]
    scratch = [
        pltpu.VMEM((rows, IN_W_PAD), F32),
        pltpu.VMEM((nb, SUBLANES, CONV_CH), F32),
        pltpu.VMEM((nb, SUBLANES, QKV_W), F32),
        pltpu.VMEM((rows, QKV_W), F32),
        pltpu.VMEM((rows, LANES), F32),
        pltpu.VMEM((rows, LANES), F32),
        pltpu.VMEM((rows, GDN_HEADS * GDN_DV), F32),
        pltpu.VMEM(ssm_shape, F32),
        pltpu.VMEM((rows, D_MODEL), BF16),
    ]
    return pl.pallas_call(
        functools.partial(_layer_kernel, cfg),
        grid=(n_tiles + (1 if cfg.pipelined else 0),),
        in_specs=in_specs,
        out_specs=out_specs,
        out_shape=out_shape,
        scratch_shapes=scratch,
        compiler_params=pltpu.CompilerParams(
            dimension_semantics=("arbitrary",), vmem_limit_bytes=VMEM_LIMIT_BYTES),
        name=name,
    )(x, x, tail_a, tail_q, ssm, *params)


META_PAD_LEN = 128
PROMPT_TILE = 256
SAMPLE_PAD_LEN = 8
SAMPLE_SEQS_PER_STEP = 16


def kernel(x_prompt, x_sample, state_conv_a, state_conv_qkv, state_ssm, meta_tokens, norm_mix, w_in, conv_a_w,
           conv_qkv_w, a_log, dt_bias, gdn_norm, w_out, norm_ffn, w_up, w_down, norm_final):
    assert norm_mix.shape[0] == 1, "single layer"
    bp, tp, _ = x_prompt.shape
    bs, ts, _ = x_sample.shape
    n_meta = meta_tokens.shape[0]

    def row(v):
        return v.reshape(1, -1).astype(F32)

    def lane_row(v):
        return jnp.pad(v.astype(F32), (0, LANES - v.shape[0])).reshape(1, LANES)

    params = (
        row(norm_mix[0]),
        _bf(w_in[0, :, :O_AB]), _bf(jnp.pad(w_in[0, :, O_AB:], ((0, 0), (0, IN_W_PAD - w_in.shape[2])))),
        conv_a_w[0].astype(F32), conv_qkv_w[0].astype(F32),
        lane_row(a_log[0]), lane_row(dt_bias[0]), row(gdn_norm[0]),
        _bf(w_out[0]), row(norm_ffn[0]),
        _bf(w_up[0]),
        _bf(w_down[0]).reshape(FFN_HIDDEN // FFN_CHUNK, FFN_CHUNK, D_MODEL),
        row(norm_final),
    )

    meta_cfg = Cfg(batch=1, nb=1, seq_len=META_PAD_LEN, valid_len=n_meta, chunk=64, n_tt=1, shared_state=False,
                   pipelined=False)
    meta_x = meta_tokens.astype(F32)[None]
    _, ta_m, tq_m, ssm_m = _layer_call(
        meta_cfg, "layer_meta", meta_x,
        jnp.zeros((1, SCONV_K - 1, CONV_CH), F32), jnp.zeros((1, GDN_CONV_K - 1, QKV_W), F32),
        jnp.zeros((1, GDN_HEADS, GDN_DK, GDN_DV), F32), params)

    prompt_cfg = Cfg(batch=bp, nb=1, seq_len=PROMPT_TILE, valid_len=PROMPT_TILE, chunk=64,
                     n_tt=tp // PROMPT_TILE, shared_state=True, pipelined=True)
    y_p, ta_p, tq_p, ssm_p = _layer_call(prompt_cfg, "layer_prompt", x_prompt, ta_m, tq_m, ssm_m, params)

    sample_cfg = Cfg(batch=bs, nb=SAMPLE_SEQS_PER_STEP, seq_len=SAMPLE_PAD_LEN, valid_len=ts, chunk=SAMPLE_PAD_LEN,
                     n_tt=1, shared_state=False, pipelined=False)
    y_s, ta_s, tq_s, ssm_s = _layer_call(
        sample_cfg, "layer_sample", x_sample, state_conv_a[0], state_conv_qkv[0], state_ssm[0], params)

    return (y_p, y_s, ta_p[None], tq_p[None], ssm_p[None], ta_s[None], tq_s[None], ssm_s[None])
```

```python
import dataclasses
import functools

import jax
import jax.numpy as jnp
from jax import lax
from jax.experimental import pallas as pl
from jax.experimental.pallas import tpu as pltpu

F32 = jnp.float32
BF16 = jnp.bfloat16

D_MODEL = 1024
CONV_CH = 512
SCONV_K = 3
GDN_HEADS = 4
GDN_DK = 128
GDN_DV = 128
GDN_CONV_K = 4
QKV_W = GDN_HEADS * (2 * GDN_DK + GDN_DV)
FFN_HIDDEN = 4 * D_MODEL
EPS = 1e-6

O_B = 0
O_C = O_B + CONV_CH
O_H = O_C + CONV_CH
O_QKV = O_H + CONV_CH
O_G = O_QKV + QKV_W
O_AB = O_G + GDN_HEADS * GDN_DV
LANES = 128
SUBLANES = 8
IN_W_PAD = O_AB + LANES
VC_ROWS = 128
NEUMANN_BASE = 16
FFN_CHUNK = 256
VMEM_LIMIT_BYTES = 56 * 1024 * 1024

Q_COL = 0
K_COL = GDN_HEADS * GDN_DK
V_COL = 2 * GDN_HEADS * GDN_DK


@dataclasses.dataclass(frozen=True)
class Cfg:
    batch: int
    nb: int
    seq_len: int
    valid_len: int
    chunk: int
    n_tt: int
    shared_state: bool
    pipelined: bool

    @property
    def rows(self):
        return self.nb * self.seq_len

    @property
    def n_tiles(self):
        return (self.batch // self.nb) * self.n_tt

    @property
    def x_len(self):
        return self.valid_len if self.n_tt == 1 else self.seq_len

    @property
    def carry(self):
        return self.nb == 1


def _bf(x):
    return x.astype(BF16)


def _dot(a, b):
    return jnp.dot(a, b, preferred_element_type=F32)


def _dot_nt(a, b):
    return lax.dot_general(a, b, (((1,), (1,)), ((), ())), preferred_element_type=F32)


def _rms(x, g_row):
    return x * lax.rsqrt(jnp.mean(x * x, axis=-1, keepdims=True) + EPS) * g_row


def _softplus(x):
    return jnp.maximum(x, 0.0) + jnp.log1p(jnp.exp(-jnp.abs(x)))


def _silu(x):
    return x * jax.nn.sigmoid(x)


def _log2(n):
    k = n.bit_length() - 1
    assert (1 << k) == n, n
    return k


def _halves(x):
    return x[:, :LANES], x[:, LANES:]


def _pair(a, b):
    return jnp.concatenate([a, b], axis=1)


def _block_diag(y):
    ya, yb = _halves(y)
    z = jnp.zeros_like(ya)
    return jnp.concatenate([_pair(ya, z), _pair(z, yb)], axis=0)


def _pmm(x, y):
    return _dot(_bf(x), _block_diag(_bf(y)))


def _each(f, *lists):
    return [f(*xs) for xs in zip(*lists)]


def _problems(cfg):
    c = cfg.chunk
    if cfg.carry:
        assert 2 * c == VC_ROWS and GDN_HEADS == 4
        n = -(-cfg.valid_len // c)
        return [([(u * c, c, 0), (u * c, c, 2)], [(u * c, c, 1), (u * c, c, 3)]) for u in range(n)]
    half = VC_ROWS // GDN_HEADS
    assert half % cfg.seq_len == 0 and cfg.seq_len == c and cfg.rows % (2 * half) == 0
    return [([(u * 2 * half, half, h) for h in range(GDN_HEADS)],
             [(u * 2 * half + half, half, h) for h in range(GDN_HEADS)]) for u in range(cfg.rows // (2 * half))]


@dataclasses.dataclass
class Prep:
    u: jax.Array
    w: jax.Array
    qk: jax.Array
    q_dec: jax.Array
    k_dec: jax.Array
    g_last: tuple


def _delta_prep(cfg, problems, qkv_scr, gb_scr, gc_scr, out):
    c = cfg.chunk
    lc = _log2(c)
    ii = lax.broadcasted_iota(jnp.int32, (VC_ROWS, 2 * LANES), 0)
    jj = lax.broadcasted_iota(jnp.int32, (VC_ROWS, 2 * LANES), 1) & (LANES - 1)
    same = (ii >> lc) == (jj >> lc)
    lower = same & (jj <= ii)
    strict = same & (jj < ii)
    end_col = (jj == (ii | (c - 1)))[:, :LANES]
    eye = jnp.where(ii == jj, 1.0, 0.0)

    def stacked(ref, col, width, half):
        return jnp.concatenate([ref[r0:r0 + n, col(h):col(h) + width] for (r0, n, h) in half], axis=0)

    def packed(ref, base):
        return [_pair(*[stacked(ref, lambda h: base + LANES * h, LANES, half) for half in pr]) for pr in problems]

    def columns(ref, base):
        return [[stacked(ref, lambda h: base + h, 1, half) for half in pr] for pr in problems]

    def spread(cols):
        return _pair(*[jnp.broadcast_to(col, (VC_ROWS, LANES)) for col in cols])

    q, k, v = packed(qkv_scr, Q_COL), packed(qkv_scr, K_COL), packed(qkv_scr, V_COL)
    beta_c, gcum_c = columns(gb_scr, GDN_HEADS), columns(gc_scr, 0)
    beta, gcum = _each(spread, beta_c), _each(spread, gcum_c)
    g_rows = _each(lambda g: _pair(*[h.T for h in _halves(g)]), gcum)
    decay = _each(lambda g, gr: jnp.where(lower, jnp.exp(g - gr), 0.0), gcum, g_rows)
    g_last = _each(lambda gr: tuple(jnp.sum(jnp.where(end_col, h, 0.0), axis=1, keepdims=True) for h in _halves(gr)),
                   g_rows)
    kq = _each(lambda kk, qq: _dot_nt(_bf(jnp.concatenate([kk, qq], axis=0)), _block_diag(_bf(kk))), k, q)
    qk = _each(lambda m, d: m[VC_ROWS:] * d, kq, decay)
    a = _each(lambda m, b, d: jnp.where(strict, m[:VC_ROWS] * b * d, 0.0), kq, beta, decay)
    yield

    base = min(c, NEUMANN_BASE)
    lb = _log2(base)
    p = _each(lambda x: jnp.where((ii >> lb) == (jj >> lb), x, 0.0), a)
    t = _each(lambda x: eye - x, p)
    for _ in range(lb - 1):
        p = _each(_pmm, p, p)
        t = _each(lambda x, y: x + _pmm(x, y), t, p)
        yield
    size = base
    while size < c:
        ls = _log2(size)
        couple = ((ii >> (ls + 1)) == (jj >> (ls + 1))) & ((ii >> ls) != (jj >> ls))
        off = _each(lambda x: jnp.where(couple, x, 0.0), a)
        t_off = _each(_pmm, t, off)
        t = _each(lambda x, y: x - _pmm(y, x), t, t_off)
        size *= 2
        yield

    exp_g = _each(lambda cols: spread([jnp.exp(col) for col in cols]), gcum_c)
    k_scale = _each(lambda cols, gl: spread([jnp.exp(e - col) for col, e in zip(cols, gl)]), gcum_c, g_last)
    u = _each(lambda x, vv, b: _pmm(x, vv * b), t, v, beta)
    w = _each(lambda x, kk, b, e: _pmm(x, kk * (b * e)), t, k, beta, exp_g)
    q_dec = _each(lambda x, e: x * e, q, exp_g)
    k_dec = _each(lambda x, s: x * s, k, k_scale)
    out["preps"] = [Prep(*xs) for xs in zip(u, w, qk, q_dec, k_dec, g_last)]


def _delta_apply_carry(cfg, problems, preps, s_scr, o_scr):
    c = cfg.chunk
    top_rows = lax.broadcasted_iota(jnp.int32, (VC_ROWS, GDN_DV), 0) < c
    for (half_a, half_b), pr in zip(problems, preps):
        state = [s_scr[h] for h in range(GDN_HEADS)]
        w_s, q_s = [], []
        for blk in range(VC_ROWS // c):
            lo = blk * c
            wq = _bf(jnp.concatenate([pr.w[lo:lo + c], pr.q_dec[lo:lo + c]], axis=0))
            r = _dot(wq, _block_diag(_bf(_pair(state[half_a[blk][2]], state[half_b[blk][2]]))))
            w_s.append(r[:c])
            q_s.append(r[c:])
        v_new = pr.u - jnp.concatenate(w_s, axis=0)
        o = jnp.concatenate(q_s, axis=0) + _pmm(pr.qk, v_new)
        for half, vn, kd, gl, o_h in zip((half_a, half_b), _halves(v_new), _halves(pr.k_dec), pr.g_last, _halves(o)):
            upd = _dot(_bf(kd.T), _bf(_pair(jnp.where(top_rows, vn, 0.0), jnp.where(top_rows, 0.0, vn))))
            for blk, (r0, n, h) in enumerate(half):
                lo = blk * c
                s_scr[h] = state[h] * jnp.exp(gl[lo:lo + 1, :]) + upd[:, GDN_DV * blk:GDN_DV * (blk + 1)]
                o_scr[r0:r0 + n, GDN_DV * h:GDN_DV * (h + 1)] = o_h[lo:lo + c]


def _delta_apply_blocks(cfg, problems, preps, ssm_ref, nssm_ref, o_scr):
    c = cfg.chunk
    lc = _log2(c)
    blk_id = lax.broadcasted_iota(jnp.int32, (VC_ROWS, GDN_DV), 0) >> lc
    for pieces, pr in zip(problems, preps):
        halves = []
        for half, w_h, qd_h, u_h in zip(pieces, _halves(pr.w), _halves(pr.q_dec), _halves(pr.u)):
            blocks = [(r0 + t0, h, (r0 + t0) // cfg.seq_len) for (r0, n, h) in half for t0 in range(0, n, c)]
            vs, qs = [], []
            for b, (_, h, seq) in enumerate(blocks):
                lo = b * c
                r = _dot(_bf(jnp.concatenate([w_h[lo:lo + c], qd_h[lo:lo + c]], axis=0)), _bf(ssm_ref[seq, h]))
                vs.append(u_h[lo:lo + c] - r[:c])
                qs.append(r[c:])
            halves.append((blocks, jnp.concatenate(vs, axis=0), jnp.concatenate(qs, axis=0)))
        v_new = _pair(halves[0][1], halves[1][1])
        o = _pair(halves[0][2], halves[1][2]) + _pmm(pr.qk, v_new)
        for (blocks, vn, _), kd, gl, o_h in zip(halves, _halves(pr.k_dec), pr.g_last, _halves(o)):
            kd_t = _bf(kd.T)
            for b0 in range(0, len(blocks), 2):
                upd = _dot(kd_t, _bf(_pair(jnp.where(blk_id == b0, vn, 0.0), jnp.where(blk_id == b0 + 1, vn, 0.0))))
                for b in (b0, b0 + 1):
                    row, h, seq = blocks[b]
                    lo = b * c
                    nssm_ref[seq, h] = (ssm_ref[seq, h] * jnp.exp(gl[lo:lo + 1, :])
                                        + upd[:, GDN_DV * (b - b0):GDN_DV * (b - b0 + 1)])
                    o_scr[row:row + c, GDN_DV * h:GDN_DV * (h + 1)] = o_h[lo:lo + c]


def _layer_kernel(cfg, x_ref, xb_ref, ta_ref, tq_ref, ssm_ref, nmix_ref, win_ref, wab_ref, caw_ref, cqw_ref, alog_ref, dt_ref,
                  gnorm_ref, wout_ref, nffn_ref, wup_ref, wdown_ref, nfin_ref,
                  y_ref, nta_ref, ntq_ref, nssm_ref,
                  z_scr, xa_buf, xq_buf, qkv_scr, gb_scr, gc_scr, o_scr, s_scr, mix_scr, up_scr):
    step = pl.program_id(0)
    nb, sl, rows = cfg.nb, cfg.seq_len, cfg.rows
    if cfg.pipelined:
        tile = jnp.minimum(step, cfg.n_tiles - 1)
        real = step < cfg.n_tiles
    else:
        tile = step
        real = True
    j = lax.rem(tile, cfg.n_tt)
    first = j == 0
    last = (j == cfg.n_tt - 1) & real

    @pl.when(first)
    def _():
        xa_buf[...] = jnp.zeros(xa_buf.shape, F32)
        xq_buf[...] = jnp.zeros(xq_buf.shape, F32)
        xa_buf[:, SUBLANES - (SCONV_K - 1):, :] = ta_ref[...]
        xq_buf[:, SUBLANES - (GDN_CONV_K - 1):, :] = tq_ref[...]
        if cfg.carry:
            s_scr[...] = ssm_ref[0]

    st = {}
    problems = _problems(cfg)

    def padded_rows(ref):
        x = ref[...]
        if cfg.x_len < sl:
            x = jnp.concatenate([x, jnp.zeros((nb, sl - cfg.x_len, D_MODEL), F32)], axis=1)
        return x.reshape(rows, D_MODEL)

    def proj(lo, hi):
        z_scr[:, lo:hi] = _dot(st["xn"], win_ref[:, lo:hi])

    if cfg.pipelined:
        @pl.when(step == 0)
        def _():
            mix_scr[...] = jnp.zeros(mix_scr.shape, mix_scr.dtype)

    tail_lo = (cfg.valid_len - 1) // SUBLANES * SUBLANES
    assert cfg.valid_len - tail_lo >= GDN_CONV_K - 1 and (cfg.n_tt == 1 or cfg.valid_len == sl)

    def causal_conv(x, prev_buf, w_ref, col, width):
        taps = w_ref.shape[0]
        t8 = lax.broadcasted_iota(jnp.int32, (nb, SUBLANES, width), 1)
        prev = prev_buf[:, :, col:col + width]
        acc = x * w_ref[taps - 1:taps, col:col + width]
        for back in range(1, taps):
            moved = pltpu.roll(x, back, 1)
            head = jnp.where(t8 < back, pltpu.roll(prev, back, 1), moved[:, :SUBLANES, :])
            moved = head if sl == SUBLANES else jnp.concatenate([head, moved[:, SUBLANES:, :]], axis=1)
            acc = acc + moved * w_ref[taps - 1 - back:taps - back, col:col + width]
        prev_buf[:, :, col:col + width] = x[:, tail_lo:tail_lo + SUBLANES, :]
        return acc

    def conv_q(col):
        width = GDN_HEADS * GDN_DK
        x = z_scr[:, O_QKV + col:O_QKV + col + width].reshape(nb, sl, width)
        act = _silu(causal_conv(x, xq_buf, cqw_ref, col, width).reshape(rows, width))
        if col == V_COL:
            qkv_scr[:, col:col + width] = act
            return
        scale = GDN_DK ** -0.5 if col == Q_COL else 1.0
        for h in range(GDN_HEADS):
            ah = act[:, GDN_DK * h:GDN_DK * (h + 1)]
            qkv_scr[:, col + GDN_DK * h:col + GDN_DK * (h + 1)] = (
                ah * (lax.rsqrt(jnp.sum(ah * ah, axis=-1, keepdims=True) + EPS) * scale))

    def conv_a():
        x = (z_scr[:, O_C:O_H] * z_scr[:, O_H:O_QKV]).reshape(nb, sl, CONV_CH)
        acc = causal_conv(x, xa_buf, caw_ref, 0, CONV_CH)
        mix_scr[:, :CONV_CH] = _bf(z_scr[:, O_B:O_C] * acc.reshape(rows, CONV_CH))

    def gates():
        zab = z_scr[:, O_AB:IN_W_PAD]
        g = -jnp.exp(alog_ref[...]) * _softplus(zab + dt_ref[...])
        beta = jax.nn.sigmoid(zab)
        lane = lax.broadcasted_iota(jnp.int32, (rows, LANES), 1)
        gb = jnp.where(lane < GDN_HEADS, g, beta)
        if cfg.valid_len < sl:
            t_in_seq = lax.broadcasted_iota(jnp.int32, (rows, LANES), 0) & (sl - 1)
            gb = jnp.where(t_in_seq < cfg.valid_len, gb, 0.0)
        gb_scr[...] = gb
        ri = lax.broadcasted_iota(jnp.int32, (rows, rows), 0)
        rj = lax.broadcasted_iota(jnp.int32, (rows, rows), 1)
        lc = _log2(cfg.chunk)
        tri = jnp.where(((ri >> lc) == (rj >> lc)) & (rj <= ri), 1.0, 0.0).astype(BF16)
        g1 = _bf(gb)
        r1 = gb - g1.astype(F32)
        g2 = _bf(r1)
        g3 = _bf(r1 - g2.astype(F32))
        gc_scr[...] = _dot(tri, g1) + (_dot(tri, g2) + _dot(tri, g3))

    def gated_norm(lo, hi):
        for h in range(GDN_HEADS):
            oh = o_scr[lo:hi, GDN_DV * h:GDN_DV * (h + 1)]
            gate = z_scr[lo:hi, O_G + GDN_DV * h:O_G + GDN_DV * (h + 1)]
            mix_scr[lo:hi, CONV_CH + GDN_DV * h:CONV_CH + GDN_DV * (h + 1)] = _bf(
                _rms(oh, gnorm_ref[...]) * _silu(gate))

    def project(src_ref, groups):
        if "xn" not in st:
            st["xn"] = _bf(_rms(padded_rows(src_ref), nmix_ref[...]))
        for group in groups:
            if group == 0:
                proj(O_QKV, O_G)
            elif group == 1:
                proj(O_B, O_QKV)
            else:
                proj(O_G, O_AB)
                z_scr[:, O_AB:] = _dot(st["xn"], wab_ref[...])

    def front_half():
        project(x_ref, (0, 1))
        yield 0
        conv_q(Q_COL)
        yield 0
        project(x_ref, (2,))
        yield 0
        conv_q(K_COL)
        yield 3
        conv_q(V_COL)
        yield 2
        conv_a()
        yield 3
        gates()
        yield 2
        for i, _ in enumerate(_delta_prep(cfg, problems, qkv_scr, gb_scr, gc_scr, st)):
            yield 2 if i == 0 else 1
        per = cfg.chunk if cfg.carry else 2 * (VC_ROWS // GDN_HEADS)
        done = len(problems) * per
        if done < rows:
            o_scr[done:, :] = jnp.zeros((rows - done, GDN_HEADS * GDN_DV), F32)
            gated_norm(done, rows)
        yield 0
        for i in range(len(problems)):
            if cfg.carry:
                _delta_apply_carry(cfg, problems[i:i + 1], st["preps"][i:i + 1], s_scr, o_scr)
            else:
                _delta_apply_blocks(cfg, problems[i:i + 1], st["preps"][i:i + 1], ssm_ref, nssm_ref, o_scr)
            gated_norm(i * per, (i + 1) * per)
            yield 1

    def back_half():
        h1 = padded_rows(xb_ref) + _dot(mix_scr[...], wout_ref[...])
        hn = _bf(_rms(h1, nffn_ref[...]))
        yield
        for c in range(FFN_HIDDEN // FFN_CHUNK):
            up = jnp.maximum(_dot(hn, wup_ref[:, c * FFN_CHUNK:(c + 1) * FFN_CHUNK]), 0.0)
            up_scr[:, c * FFN_CHUNK:(c + 1) * FFN_CHUNK] = _bf(up * up)
            yield
        outs = []
        for n in range(D_MODEL // FFN_CHUNK):
            cols = slice(n * FFN_CHUNK, (n + 1) * FFN_CHUNK)
            outs.append(h1[:, cols] + _dot(up_scr[...], wdown_ref[:, cols]))
            yield
        h2 = jnp.concatenate(outs, axis=1)
        y_ref[...] = _rms(h2, nfin_ref[...]).reshape(nb, sl, D_MODEL)[:, :cfg.x_len, :]

    back = back_half()
    if cfg.pipelined:
        next(back)
        for n_back in front_half():
            for _ in range(n_back):
                next(back, None)
    else:
        for _ in front_half():
            pass
    for _ in back:
        pass

    @pl.when(last)
    def _():
        end = cfg.valid_len - tail_lo
        nta_ref[...] = xa_buf[:, end - (SCONV_K - 1):end, :]
        ntq_ref[...] = xq_buf[:, end - (GDN_CONV_K - 1):end, :]
        if cfg.carry:
            nssm_ref[0] = s_scr[...]


def _layer_call(cfg, name, x, tail_a, tail_q, ssm, params):
    nb, sl, rows = cfg.nb, cfg.seq_len, cfg.rows
    assert x.shape == (cfg.batch, cfg.x_len * cfg.n_tt, D_MODEL), (x.shape, cfg)
    assert cfg.batch % nb == 0 and sl % SUBLANES == 0 and rows % VC_ROWS == 0
    assert not cfg.shared_state or nb == 1
    n_tiles, n_tt = cfg.n_tiles, cfg.n_tt

    def front_tile(s):
        return jnp.minimum(s, n_tiles - 1) if cfg.pipelined else s

    def back_tile(s):
        return jnp.maximum(s - 1, 0) if cfg.pipelined else s

    def tile_spec(which):
        return pl.BlockSpec((nb, cfg.x_len, D_MODEL), lambda s: (which(s) // n_tt, which(s) % n_tt, 0))

    def state_spec(shape):
        zeros = (0,) * len(shape)
        if cfg.shared_state:
            return pl.BlockSpec((nb,) + shape, lambda s: (0,) + zeros)
        return pl.BlockSpec((nb,) + shape, lambda s: (front_tile(s) // n_tt,) + zeros)

    def out_state_spec(shape):
        zeros = (0,) * len(shape)
        return pl.BlockSpec((nb,) + shape, lambda s: (front_tile(s) // n_tt,) + zeros)

    def const_spec(arr):
        zeros = (0,) * arr.ndim
        return pl.BlockSpec(arr.shape, lambda s: zeros, pipeline_mode=pl.Buffered(1))

    ta_shape, tq_shape = (SCONV_K - 1, CONV_CH), (GDN_CONV_K - 1, QKV_W)
    ssm_shape = (GDN_HEADS, GDN_DK, GDN_DV)
    in_specs = [tile_spec(front_tile), tile_spec(back_tile),
                state_spec(ta_shape), state_spec(tq_shape), state_spec(ssm_shape)]
    in_specs += [const_spec(p) for p in params]
    out_specs = [tile_spec(back_tile), out_state_spec(ta_shape), out_state_spec(tq_shape), out_state_spec(ssm_shape)]
    out_shape = [
        jax.ShapeDtypeStruct(x.shape, F32),
        jax.ShapeDtypeStruct((cfg.batch,) + ta_shape, F32),
        jax.ShapeDtypeStruct((cfg.batch,) + tq_shape, F32),
        jax.ShapeDtypeStruct((cfg.batch,) + ssm_shape, F32),
    ]
    scratch = [
        pltpu.VMEM((rows, IN_W_PAD), F32),
        pltpu.VMEM((nb, SUBLANES, CONV_CH), F32),
        pltpu.VMEM((nb, SUBLANES, QKV_W), F32),
        pltpu.VMEM((rows, QKV_W), F32),
        pltpu.VMEM((rows, LANES), F32),
        pltpu.VMEM((rows, LANES), F32),
        pltpu.VMEM((rows, GDN_HEADS * GDN_DV), F32),
        pltpu.VMEM(ssm_shape, F32),
        pltpu.VMEM((rows, D_MODEL), BF16),
        pltpu.VMEM((rows, FFN_HIDDEN), BF16),
    ]
    return pl.pallas_call(
        functools.partial(_layer_kernel, cfg),
        grid=(n_tiles + (1 if cfg.pipelined else 0),),
        in_specs=in_specs,
        out_specs=out_specs,
        out_shape=out_shape,
        scratch_shapes=scratch,
        compiler_params=pltpu.CompilerParams(
            dimension_semantics=("arbitrary",), vmem_limit_bytes=VMEM_LIMIT_BYTES),
        name=name,
    )(x, x, tail_a, tail_q, ssm, *params)


META_PAD_LEN = 128
PROMPT_TILE = 256
SAMPLE_PAD_LEN = 8
SAMPLE_SEQS_PER_STEP = 16


def kernel(x_prompt, x_sample, state_conv_a, state_conv_qkv, state_ssm, meta_tokens, norm_mix, w_in, conv_a_w,
           conv_qkv_w, a_log, dt_bias, gdn_norm, w_out, norm_ffn, w_up, w_down, norm_final):
    assert norm_mix.shape[0] == 1, "single layer"
    bp, tp, _ = x_prompt.shape
    bs, ts, _ = x_sample.shape
    n_meta = meta_tokens.shape[0]

    def row(v):
        return v.reshape(1, -1).astype(F32)

    def lane_row(v):
        return jnp.pad(v.astype(F32), (0, LANES - v.shape[0])).reshape(1, LANES)

    params = (
        row(norm_mix[0]),
        _bf(w_in[0, :, :O_AB]), _bf(jnp.pad(w_in[0, :, O_AB:], ((0, 0), (0, IN_W_PAD - w_in.shape[2])))),
        conv_a_w[0].astype(F32), conv_qkv_w[0].astype(F32),
        lane_row(a_log[0]), lane_row(dt_bias[0]), row(gdn_norm[0]),
        _bf(w_out[0]), row(norm_ffn[0]),
        _bf(w_up[0]),
        _bf(w_down[0]),
        row(norm_final),
    )

    meta_cfg = Cfg(batch=1, nb=1, seq_len=META_PAD_LEN, valid_len=n_meta, chunk=64, n_tt=1, shared_state=False,
                   pipelined=False)
    meta_x = meta_tokens.astype(F32)[None]
    _, ta_m, tq_m, ssm_m = _layer_call(
        meta_cfg, "layer_meta", meta_x,
        jnp.zeros((1, SCONV_K - 1, CONV_CH), F32), jnp.zeros((1, GDN_CONV_K - 1, QKV_W), F32),
        jnp.zeros((1, GDN_HEADS, GDN_DK, GDN_DV), F32), params)

    prompt_cfg = Cfg(batch=bp, nb=1, seq_len=PROMPT_TILE, valid_len=PROMPT_TILE, chunk=64,
                     n_tt=tp // PROMPT_TILE, shared_state=True, pipelined=True)
    y_p, ta_p, tq_p, ssm_p = _layer_call(prompt_cfg, "layer_prompt", x_prompt, ta_m, tq_m, ssm_m, params)

    sample_cfg = Cfg(batch=bs, nb=SAMPLE_SEQS_PER_STEP, seq_len=SAMPLE_PAD_LEN, valid_len=ts, chunk=SAMPLE_PAD_LEN,
                     n_tt=1, shared_state=False, pipelined=False)
    y_s, ta_s, tq_s, ssm_s = _layer_call(
        sample_cfg, "layer_sample", x_sample, state_conv_a[0], state_conv_qkv[0], state_ssm[0], params)

    return (y_p, y_s, ta_p[None], tq_p[None], ssm_p[None], ta_s[None], tq_s[None], ssm_s[None])
```

```python
import dataclasses
import functools

import jax
import jax.numpy as jnp
from jax import lax
from jax.experimental import pallas as pl
from jax.experimental.pallas import tpu as pltpu

F32 = jnp.float32
BF16 = jnp.bfloat16

D_MODEL = 1024
CONV_CH = 512
SCONV_K = 3
GDN_HEADS = 4
GDN_DK = 128
GDN_DV = 128
GDN_CONV_K = 4
QKV_W = GDN_HEADS * (2 * GDN_DK + GDN_DV)
FFN_HIDDEN = 4 * D_MODEL
EPS = 1e-6

O_B = 0
O_C = O_B + CONV_CH
O_H = O_C + CONV_CH
O_QKV = O_H + CONV_CH
O_G = O_QKV + QKV_W
O_AB = O_G + GDN_HEADS * GDN_DV
LANES = 128
SUBLANES = 8
IN_W_PAD = O_AB + LANES
VC_ROWS = 128
NEUMANN_BASE = 16
FFN_CHUNK = 256
BACK_PIECES_AFTER = (0, 0, 0, 3, 2, 3, 2, 2, 1, 1, 1, 1, 1, 0, 1, 1, 1, 1)
VMEM_LIMIT_BYTES = 56 * 1024 * 1024

Q_COL = 0
K_COL = GDN_HEADS * GDN_DK
V_COL = 2 * GDN_HEADS * GDN_DK


@dataclasses.dataclass(frozen=True)
class Cfg:
    batch: int
    nb: int
    seq_len: int
    valid_len: int
    chunk: int
    n_tt: int
    shared_state: bool
    pipelined: bool
    states_only: bool

    @property
    def rows(self):
        return self.nb * self.seq_len

    @property
    def n_tiles(self):
        return (self.batch // self.nb) * self.n_tt

    @property
    def x_len(self):
        return self.valid_len if self.n_tt == 1 else self.seq_len

    @property
    def carry(self):
        return self.nb == 1


def _bf(x):
    return x.astype(BF16)


def _dot(a, b):
    return jnp.dot(a, b, preferred_element_type=F32)


def _dot_nt(a, b):
    return lax.dot_general(a, b, (((1,), (1,)), ((), ())), preferred_element_type=F32)


def _rms(x, g_row):
    return x * lax.rsqrt(jnp.mean(x * x, axis=-1, keepdims=True) + EPS) * g_row


def _softplus(x):
    return jnp.maximum(x, 0.0) + jnp.log(1.0 + jnp.exp(-jnp.abs(x)))


def _silu(x):
    return x * jax.nn.sigmoid(x)


def _log2(n):
    k = n.bit_length() - 1
    assert (1 << k) == n, n
    return k


def _halves(x):
    return x[:, :LANES], x[:, LANES:]


def _pair(a, b):
    return jnp.concatenate([a, b], axis=1)


def _block_diag(y):
    ya, yb = _halves(y)
    z = jnp.zeros_like(ya)
    return jnp.concatenate([_pair(ya, z), _pair(z, yb)], axis=0)


def _pmm(x, y):
    return _dot(_bf(x), _block_diag(_bf(y)))


def _each(f, *lists):
    return [f(*xs) for xs in zip(*lists)]


def _problems(cfg):
    c = cfg.chunk
    if cfg.carry:
        assert 2 * c == VC_ROWS and GDN_HEADS == 4
        n = -(-cfg.valid_len // c)
        return [([(u * c, c, 0), (u * c, c, 2)], [(u * c, c, 1), (u * c, c, 3)]) for u in range(n)]
    half = VC_ROWS // GDN_HEADS
    assert half % cfg.seq_len == 0 and cfg.seq_len == c and cfg.rows % (2 * half) == 0
    return [([(u * 2 * half, half, h) for h in range(GDN_HEADS)],
             [(u * 2 * half + half, half, h) for h in range(GDN_HEADS)]) for u in range(cfg.rows // (2 * half))]


@dataclasses.dataclass
class Prep:
    u: jax.Array
    w: jax.Array
    qk: jax.Array
    q_dec: jax.Array
    k_dec: jax.Array
    g_last: tuple


def _delta_prep(cfg, problems, qkv_scr, gb_scr, gc_scr, out):
    c = cfg.chunk
    lc = _log2(c)
    ii = lax.broadcasted_iota(jnp.int32, (VC_ROWS, 2 * LANES), 0)
    jj = lax.broadcasted_iota(jnp.int32, (VC_ROWS, 2 * LANES), 1) & (LANES - 1)
    same = (ii >> lc) == (jj >> lc)
    lower = same & (jj <= ii)
    strict = same & (jj < ii)
    end_col = (jj == (ii | (c - 1)))[:, :LANES]
    eye = jnp.where(ii == jj, 1.0, 0.0)

    def stacked(ref, col, width, half):
        return jnp.concatenate([ref[r0:r0 + n, col(h):col(h) + width] for (r0, n, h) in half], axis=0)

    def packed(ref, base):
        return [_pair(*[stacked(ref, lambda h: base + LANES * h, LANES, half) for half in pr]) for pr in problems]

    def columns(ref, base):
        return [[stacked(ref, lambda h: base + h, 1, half) for half in pr] for pr in problems]

    def spread(cols):
        return _pair(*[jnp.broadcast_to(col, (VC_ROWS, LANES)) for col in cols])

    q, k, v = packed(qkv_scr, Q_COL), packed(qkv_scr, K_COL), packed(qkv_scr, V_COL)
    beta_c, gcum_c = columns(gb_scr, GDN_HEADS), columns(gc_scr, 0)
    beta, gcum = _each(spread, beta_c), _each(spread, gcum_c)
    g_rows = _each(lambda g: _pair(*[h.T for h in _halves(g)]), gcum)
    decay = _each(lambda g, gr: jnp.where(lower, jnp.exp(g - gr), 0.0), gcum, g_rows)
    g_last = _each(lambda gr: tuple(jnp.sum(jnp.where(end_col, h, 0.0), axis=1, keepdims=True) for h in _halves(gr)),
                   g_rows)
    kq = _each(lambda kk, qq: _dot_nt(_bf(jnp.concatenate([kk, qq], axis=0)), _block_diag(_bf(kk))), k, q)
    qk = _each(lambda m, d: m[VC_ROWS:] * d, kq, decay)
    a = _each(lambda m, b, d: jnp.where(strict, m[:VC_ROWS] * b * d, 0.0), kq, beta, decay)
    yield

    base = min(c, NEUMANN_BASE)
    lb = _log2(base)
    p = _each(lambda x: jnp.where((ii >> lb) == (jj >> lb), x, 0.0), a)
    t = _each(lambda x: eye - x, p)
    for _ in range(lb - 1):
        p = _each(_pmm, p, p)
        t = _each(lambda x, y: x + _pmm(x, y), t, p)
        yield
    size = base
    while size < c:
        ls = _log2(size)
        couple = ((ii >> (ls + 1)) == (jj >> (ls + 1))) & ((ii >> ls) != (jj >> ls))
        off = _each(lambda x: jnp.where(couple, x, 0.0), a)
        t_off = _each(_pmm, t, off)
        t = _each(lambda x, y: x - _pmm(y, x), t, t_off)
        size *= 2
        yield

    exp_g = _each(lambda cols: spread([jnp.exp(col) for col in cols]), gcum_c)
    k_scale = _each(lambda cols, gl: spread([jnp.exp(e - col) for col, e in zip(cols, gl)]), gcum_c, g_last)
    u = _each(lambda x, vv, b: _pmm(x, vv * b), t, v, beta)
    w = _each(lambda x, kk, b, e: _pmm(x, kk * (b * e)), t, k, beta, exp_g)
    q_dec = _each(lambda x, e: x * e, q, exp_g)
    k_dec = _each(lambda x, s: x * s, k, k_scale)
    out["preps"] = [Prep(*xs) for xs in zip(u, w, qk, q_dec, k_dec, g_last)]


def _delta_apply_carry(cfg, problems, preps, s_scr, o_scr):
    c = cfg.chunk
    top_rows = lax.broadcasted_iota(jnp.int32, (VC_ROWS, GDN_DV), 0) < c
    for (half_a, half_b), pr in zip(problems, preps):
        state = [s_scr[h] for h in range(GDN_HEADS)]
        w_s, q_s = [], []
        for blk in range(VC_ROWS // c):
            lo = blk * c
            wq = _bf(jnp.concatenate([pr.w[lo:lo + c], pr.q_dec[lo:lo + c]], axis=0))
            r = _dot(wq, _block_diag(_bf(_pair(state[half_a[blk][2]], state[half_b[blk][2]]))))
            w_s.append(r[:c])
            q_s.append(r[c:])
        v_new = pr.u - jnp.concatenate(w_s, axis=0)
        o = jnp.concatenate(q_s, axis=0) + _pmm(pr.qk, v_new)
        for half, vn, kd, gl, o_h in zip((half_a, half_b), _halves(v_new), _halves(pr.k_dec), pr.g_last, _halves(o)):
            upd = _dot(_bf(kd.T), _bf(_pair(jnp.where(top_rows, vn, 0.0), jnp.where(top_rows, 0.0, vn))))
            for blk, (r0, n, h) in enumerate(half):
                lo = blk * c
                s_scr[h] = state[h] * jnp.exp(gl[lo:lo + 1, :]) + upd[:, GDN_DV * blk:GDN_DV * (blk + 1)]
                o_scr[r0:r0 + n, GDN_DV * h:GDN_DV * (h + 1)] = o_h[lo:lo + c]


def _delta_apply_blocks(cfg, problems, preps, ssm_ref, nssm_ref, o_scr):
    c = cfg.chunk
    lc = _log2(c)
    blk_id = lax.broadcasted_iota(jnp.int32, (VC_ROWS, GDN_DV), 0) >> lc
    for pieces, pr in zip(problems, preps):
        halves = []
        for half, w_h, qd_h, u_h in zip(pieces, _halves(pr.w), _halves(pr.q_dec), _halves(pr.u)):
            blocks = [(r0 + t0, h, (r0 + t0) // cfg.seq_len) for (r0, n, h) in half for t0 in range(0, n, c)]
            vs, qs = [], []
            for b, (_, h, seq) in enumerate(blocks):
                lo = b * c
                r = _dot(_bf(jnp.concatenate([w_h[lo:lo + c], qd_h[lo:lo + c]], axis=0)), _bf(ssm_ref[seq, h]))
                vs.append(u_h[lo:lo + c] - r[:c])
                qs.append(r[c:])
            halves.append((blocks, jnp.concatenate(vs, axis=0), jnp.concatenate(qs, axis=0)))
        v_new = _pair(halves[0][1], halves[1][1])
        o = _pair(halves[0][2], halves[1][2]) + _pmm(pr.qk, v_new)
        for (blocks, vn, _), kd, gl, o_h in zip(halves, _halves(pr.k_dec), pr.g_last, _halves(o)):
            kd_t = _bf(kd.T)
            for b0 in range(0, len(blocks), 2):
                upd = _dot(kd_t, _bf(_pair(jnp.where(blk_id == b0, vn, 0.0), jnp.where(blk_id == b0 + 1, vn, 0.0))))
                for b in (b0, b0 + 1):
                    row, h, seq = blocks[b]
                    lo = b * c
                    nssm_ref[seq, h] = (ssm_ref[seq, h] * jnp.exp(gl[lo:lo + 1, :])
                                        + upd[:, GDN_DV * (b - b0):GDN_DV * (b - b0 + 1)])
                    o_scr[row:row + c, GDN_DV * h:GDN_DV * (h + 1)] = o_h[lo:lo + c]


def _layer_kernel(cfg, x_ref, xb_ref, ta_ref, tq_ref, ssm_ref, nmix_ref, win_ref, caw_ref, cqw_ref, alog_ref, dt_ref,
                  gnorm_ref, wout_ref, nffn_ref, wup_ref, wdown_ref, nfin_ref,
                  y_ref, nta_ref, ntq_ref, nssm_ref,
                  z_scr, xa_buf, xq_buf, qkv_scr, gb_scr, gc_scr, o_scr, s_scr, mix_scr, up_scr):
    step = pl.program_id(0)
    nb, sl, rows = cfg.nb, cfg.seq_len, cfg.rows
    if cfg.pipelined:
        tile = jnp.minimum(step, cfg.n_tiles - 1)
        real = step < cfg.n_tiles
    else:
        tile = step
        real = True
    j = lax.rem(tile, cfg.n_tt)
    first = j == 0
    last = (j == cfg.n_tt - 1) & real

    @pl.when(first)
    def _():
        xa_buf[...] = jnp.zeros(xa_buf.shape, F32)
        xq_buf[...] = jnp.zeros(xq_buf.shape, F32)
        xa_buf[:, SUBLANES - (SCONV_K - 1):, :] = ta_ref[...]
        xq_buf[:, SUBLANES - (GDN_CONV_K - 1):, :] = tq_ref[...]
        if cfg.carry:
            s_scr[...] = ssm_ref[0]

    st = {}
    problems = _problems(cfg)

    def padded_rows(ref):
        x = ref[...]
        if cfg.x_len < sl:
            x = jnp.concatenate([x, jnp.zeros((nb, sl - cfg.x_len, D_MODEL), F32)], axis=1)
        return x.reshape(rows, D_MODEL)

    def proj(lo, hi):
        z_scr[:, lo:hi] = _dot(st["xn"], win_ref[:, lo:hi])

    if cfg.pipelined:
        @pl.when(step == 0)
        def _():
            mix_scr[...] = jnp.zeros(mix_scr.shape, mix_scr.dtype)

    tail_lo = (cfg.valid_len - 1) // SUBLANES * SUBLANES
    assert cfg.valid_len - tail_lo >= GDN_CONV_K - 1 and (cfg.n_tt == 1 or cfg.valid_len == sl)

    def causal_conv(x, prev_buf, w_ref, col, width):
        taps = w_ref.shape[0]
        t8 = lax.broadcasted_iota(jnp.int32, (nb, SUBLANES, width), 1)
        prev = prev_buf[:, :, col:col + width]
        acc = x * w_ref[taps - 1:taps, col:col + width]
        for back in range(1, taps):
            moved = pltpu.roll(x, back, 1)
            head = jnp.where(t8 < back, pltpu.roll(prev, back, 1), moved[:, :SUBLANES, :])
            moved = head if sl == SUBLANES else jnp.concatenate([head, moved[:, SUBLANES:, :]], axis=1)
            acc = acc + moved * w_ref[taps - 1 - back:taps - back, col:col + width]
        prev_buf[:, :, col:col + width] = x[:, tail_lo:tail_lo + SUBLANES, :]
        return acc

    def conv_q(col):
        width = GDN_HEADS * GDN_DK
        x = z_scr[:, O_QKV + col:O_QKV + col + width].reshape(nb, sl, width)
        act = _silu(causal_conv(x, xq_buf, cqw_ref, col, width).reshape(rows, width))
        if col == V_COL:
            qkv_scr[:, col:col + width] = act
            return
        scale = GDN_DK ** -0.5 if col == Q_COL else 1.0
        for h in range(GDN_HEADS):
            ah = act[:, GDN_DK * h:GDN_DK * (h + 1)]
            qkv_scr[:, col + GDN_DK * h:col + GDN_DK * (h + 1)] = (
                ah * (lax.rsqrt(jnp.sum(ah * ah, axis=-1, keepdims=True) + EPS) * scale))

    def conv_a():
        x = (z_scr[:, O_C:O_H] * z_scr[:, O_H:O_QKV]).reshape(nb, sl, CONV_CH)
        acc = causal_conv(x, xa_buf, caw_ref, 0, CONV_CH)
        mix_scr[:, :CONV_CH] = _bf(z_scr[:, O_B:O_C] * acc.reshape(rows, CONV_CH))

    def gates():
        zab = z_scr[:, O_AB:IN_W_PAD]
        g = -jnp.exp(alog_ref[...]) * _softplus(zab + dt_ref[...])
        beta = jax.nn.sigmoid(zab)
        lane = lax.broadcasted_iota(jnp.int32, (rows, LANES), 1)
        gb = jnp.where(lane < GDN_HEADS, g, beta)
        if cfg.valid_len < sl:
            t_in_seq = lax.broadcasted_iota(jnp.int32, (rows, LANES), 0) & (sl - 1)
            gb = jnp.where(t_in_seq < cfg.valid_len, gb, 0.0)
        gb_scr[...] = gb
        ri = lax.broadcasted_iota(jnp.int32, (rows, rows), 0)
        rj = lax.broadcasted_iota(jnp.int32, (rows, rows), 1)
        lc = _log2(cfg.chunk)
        tri = jnp.where(((ri >> lc) == (rj >> lc)) & (rj <= ri), 1.0, 0.0).astype(BF16)
        g1 = _bf(gb)
        r1 = gb - g1.astype(F32)
        g2 = _bf(r1)
        g3 = _bf(r1 - g2.astype(F32))
        gc_scr[...] = _dot(tri, g1) + (_dot(tri, g2) + _dot(tri, g3))

    def gated_norm(lo, hi):
        for h in range(GDN_HEADS):
            oh = o_scr[lo:hi, GDN_DV * h:GDN_DV * (h + 1)]
            gate = z_scr[lo:hi, O_G + GDN_DV * h:O_G + GDN_DV * (h + 1)]
            mix_scr[lo:hi, CONV_CH + GDN_DV * h:CONV_CH + GDN_DV * (h + 1)] = _bf(
                _rms(oh, gnorm_ref[...]) * _silu(gate))

    def project(src_ref, groups):
        if "xn" not in st:
            st["xn"] = _bf(_rms(padded_rows(src_ref), nmix_ref[...]))
        for group in groups:
            if group == 0:
                proj(O_QKV, O_G)
            elif group == 1:
                proj(O_B, O_QKV)
            else:
                proj(O_G, O_AB)
                z_ab = _dot(st["xn"], win_ref[:, O_AB:O_AB + 2 * GDN_HEADS])
                z_scr[:, O_AB:] = jnp.pad(z_ab, ((0, 0), (0, LANES - 2 * GDN_HEADS)))

    def front_half():
        project(x_ref, (0, 1))
        yield
        conv_q(Q_COL)
        yield
        project(x_ref, (2,))
        yield
        conv_q(K_COL)
        yield
        conv_q(V_COL)
        yield
        conv_a()
        yield
        gates()
        yield
        for _ in _delta_prep(cfg, problems, qkv_scr, gb_scr, gc_scr, st):
            yield
        per = cfg.chunk if cfg.carry else 2 * (VC_ROWS // GDN_HEADS)
        done = len(problems) * per
        if done < rows:
            o_scr[done:, :] = jnp.zeros((rows - done, GDN_HEADS * GDN_DV), F32)
            gated_norm(done, rows)
        yield
        for i in range(len(problems)):
            if cfg.carry:
                _delta_apply_carry(cfg, problems[i:i + 1], st["preps"][i:i + 1], s_scr, o_scr)
            else:
                _delta_apply_blocks(cfg, problems[i:i + 1], st["preps"][i:i + 1], ssm_ref, nssm_ref, o_scr)
            gated_norm(i * per, (i + 1) * per)
            yield

    def back_half():
        h1 = padded_rows(xb_ref) + _dot(mix_scr[...], wout_ref[...])
        hn = _bf(_rms(h1, nffn_ref[...]))
        yield
        for c in range(FFN_HIDDEN // FFN_CHUNK):
            up = jnp.maximum(_dot(hn, wup_ref[:, c * FFN_CHUNK:(c + 1) * FFN_CHUNK]), 0.0)
            up_scr[:, c * FFN_CHUNK:(c + 1) * FFN_CHUNK] = _bf(up * up)
            yield
        outs = []
        for n in range(D_MODEL // FFN_CHUNK):
            cols = slice(n * FFN_CHUNK, (n + 1) * FFN_CHUNK)
            outs.append(h1[:, cols] + _dot(up_scr[...], wdown_ref[:, cols]))
            yield
        h2 = jnp.concatenate(outs, axis=1)
        y_ref[...] = _rms(h2, nfin_ref[...]).reshape(nb, sl, D_MODEL)[:, :cfg.x_len, :]

    back = back_half()
    if cfg.states_only:
        back = iter(())
        y_ref[...] = jnp.zeros(y_ref.shape, F32)
    if cfg.pipelined:
        next(back)
        for i, _ in enumerate(front_half()):
            for _ in range(BACK_PIECES_AFTER[i] if i < len(BACK_PIECES_AFTER) else 0):
                next(back, None)
    else:
        for _ in front_half():
            pass
    for _ in back:
        pass

    @pl.when(last)
    def _():
        end = cfg.valid_len - tail_lo
        nta_ref[...] = xa_buf[:, end - (SCONV_K - 1):end, :]
        ntq_ref[...] = xq_buf[:, end - (GDN_CONV_K - 1):end, :]
        if cfg.carry:
            nssm_ref[0] = s_scr[...]


def _layer_call(cfg, name, x, tail_a, tail_q, ssm, params):
    nb, sl, rows = cfg.nb, cfg.seq_len, cfg.rows
    assert x.shape == (cfg.batch, cfg.x_len * cfg.n_tt, D_MODEL), (x.shape, cfg)
    assert cfg.batch % nb == 0 and sl % SUBLANES == 0 and rows % VC_ROWS == 0
    assert not cfg.shared_state or nb == 1
    n_tiles, n_tt = cfg.n_tiles, cfg.n_tt

    def front_tile(s):
        return jnp.minimum(s, n_tiles - 1) if cfg.pipelined else s

    def back_tile(s):
        return jnp.maximum(s - 1, 0) if cfg.pipelined else s

    def tile_spec(which):
        return pl.BlockSpec((nb, cfg.x_len, D_MODEL), lambda s: (which(s) // n_tt, which(s) % n_tt, 0))

    def state_spec(shape):
        zeros = (0,) * len(shape)
        if cfg.shared_state:
            return pl.BlockSpec((nb,) + shape, lambda s: (0,) + zeros)
        return pl.BlockSpec((nb,) + shape, lambda s: (front_tile(s) // n_tt,) + zeros)

    def out_state_spec(shape):
        zeros = (0,) * len(shape)
        return pl.BlockSpec((nb,) + shape, lambda s: (front_tile(s) // n_tt,) + zeros)

    def const_spec(arr):
        zeros = (0,) * arr.ndim
        return pl.BlockSpec(arr.shape, lambda s: zeros, pipeline_mode=pl.Buffered(1))

    ta_shape, tq_shape = (SCONV_K - 1, CONV_CH), (GDN_CONV_K - 1, QKV_W)
    ssm_shape = (GDN_HEADS, GDN_DK, GDN_DV)
    in_specs = [tile_spec(front_tile), tile_spec(back_tile),
                state_spec(ta_shape), state_spec(tq_shape), state_spec(ssm_shape)]
    in_specs += [pl.BlockSpec(memory_space=pl.ANY) if cfg.states_only and i >= FIRST_BACK_PARAM else const_spec(p)
                 for i, p in enumerate(params)]
    out_specs = [tile_spec(back_tile), out_state_spec(ta_shape), out_state_spec(tq_shape), out_state_spec(ssm_shape)]
    out_shape = [
        jax.ShapeDtypeStruct(x.shape, F32),
        jax.ShapeDtypeStruct((cfg.batch,) + ta_shape, F32),
        jax.ShapeDtypeStruct((cfg.batch,) + tq_shape, F32),
        jax.ShapeDtypeStruct((cfg.batch,) + ssm_shape, F32),
    ]
    scratch = [
        pltpu.VMEM((rows, IN_W_PAD), F32),
        pltpu.VMEM((nb, SUBLANES, CONV_CH), F32),
        pltpu.VMEM((nb, SUBLANES, QKV_W), F32),
        pltpu.VMEM((rows, QKV_W), F32),
        pltpu.VMEM((rows, LANES), F32),
        pltpu.VMEM((rows, LANES), F32),
        pltpu.VMEM((rows, GDN_HEADS * GDN_DV), F32),
        pltpu.VMEM(ssm_shape, F32),
        pltpu.VMEM((rows, D_MODEL), BF16),
        pltpu.VMEM((rows, FFN_HIDDEN), BF16),
    ]
    return pl.pallas_call(
        functools.partial(_layer_kernel, cfg),
        grid=(n_tiles + (1 if cfg.pipelined else 0),),
        in_specs=in_specs,
        out_specs=out_specs,
        out_shape=out_shape,
        scratch_shapes=scratch,
        compiler_params=pltpu.CompilerParams(
            dimension_semantics=("arbitrary",), vmem_limit_bytes=VMEM_LIMIT_BYTES),
        name=name,
    )(x, x, tail_a, tail_q, ssm, *params)


META_PAD_LEN = 128
FIRST_BACK_PARAM = 7
PROMPT_TILE = 256
SAMPLE_PAD_LEN = 8
SAMPLE_SEQS_PER_STEP = 16


def kernel(x_prompt, x_sample, state_conv_a, state_conv_qkv, state_ssm, meta_tokens, norm_mix, w_in, conv_a_w,
           conv_qkv_w, a_log, dt_bias, gdn_norm, w_out, norm_ffn, w_up, w_down, norm_final):
    assert norm_mix.shape[0] == 1, "single layer"
    bp, tp, _ = x_prompt.shape
    bs, ts, _ = x_sample.shape
    n_meta = meta_tokens.shape[0]

    def row(v):
        return v.reshape(1, -1).astype(F32)

    def lane_row(v):
        return jnp.pad(v.astype(F32), (0, LANES - v.shape[0])).reshape(1, LANES)

    params = (
        row(norm_mix[0]),
        _bf(w_in[0]),
        conv_a_w[0].astype(F32), conv_qkv_w[0].astype(F32),
        lane_row(a_log[0]), lane_row(dt_bias[0]), row(gdn_norm[0]),
        _bf(w_out[0]), row(norm_ffn[0]),
        _bf(w_up[0]),
        _bf(w_down[0]),
        row(norm_final),
    )

    meta_cfg = Cfg(batch=1, nb=1, seq_len=META_PAD_LEN, valid_len=n_meta, chunk=64, n_tt=1, shared_state=False,
                   pipelined=False, states_only=True)
    meta_x = meta_tokens.astype(F32)[None]
    _, ta_m, tq_m, ssm_m = _layer_call(
        meta_cfg, "layer_meta", meta_x,
        jnp.zeros((1, SCONV_K - 1, CONV_CH), F32), jnp.zeros((1, GDN_CONV_K - 1, QKV_W), F32),
        jnp.zeros((1, GDN_HEADS, GDN_DK, GDN_DV), F32), params)

    prompt_cfg = Cfg(batch=bp, nb=1, seq_len=PROMPT_TILE, valid_len=PROMPT_TILE, chunk=64,
                     n_tt=tp // PROMPT_TILE, shared_state=True, pipelined=True, states_only=False)
    y_p, ta_p, tq_p, ssm_p = _layer_call(prompt_cfg, "layer_prompt", x_prompt, ta_m, tq_m, ssm_m, params)

    sample_cfg = Cfg(batch=bs, nb=SAMPLE_SEQS_PER_STEP, seq_len=SAMPLE_PAD_LEN, valid_len=ts, chunk=SAMPLE_PAD_LEN,
                     n_tt=1, shared_state=False, pipelined=False, states_only=False)
    y_s, ta_s, tq_s, ssm_s = _layer_call(
        sample_cfg, "layer_sample", x_sample, state_conv_a[0], state_conv_qkv[0], state_ssm[0], params)

    return (y_p, y_s, ta_p[None], tq_p[None], ssm_p[None], ta_s[None], tq_s[None], ssm_s[None])
```

```python
import dataclasses
import functools

import jax
import jax.numpy as jnp
from jax import lax
from jax.experimental import pallas as pl
from jax.experimental.pallas import tpu as pltpu

F32 = jnp.float32
BF16 = jnp.bfloat16

D_MODEL = 1024
CONV_CH = 512
SCONV_K = 3
GDN_HEADS = 4
GDN_DK = 128
GDN_DV = 128
GDN_CONV_K = 4
QKV_W = GDN_HEADS * (2 * GDN_DK + GDN_DV)
FFN_HIDDEN = 4 * D_MODEL
EPS = 1e-6

O_B = 0
O_C = O_B + CONV_CH
O_H = O_C + CONV_CH
O_QKV = O_H + CONV_CH
O_G = O_QKV + QKV_W
O_AB = O_G + GDN_HEADS * GDN_DV
LANES = 128
SUBLANES = 8
IN_W_PAD = O_AB + LANES
VC_ROWS = 128
NEUMANN_BASE = 16
FFN_CHUNK = 256
PROMPT_BACK_AFTER = (0, 0, 0, 3, 2, 3, 2, 2, 1, 1, 1, 1, 1, 0, 1, 1, 1, 1)
VMEM_LIMIT_BYTES = 56 * 1024 * 1024

Q_COL = 0
K_COL = GDN_HEADS * GDN_DK
V_COL = 2 * GDN_HEADS * GDN_DK


@dataclasses.dataclass(frozen=True)
class Cfg:
    batch: int
    nb: int
    seq_len: int
    valid_len: int
    chunk: int
    n_tt: int
    shared_state: bool
    pipelined: bool
    states_only: bool
    back_after: tuple = ()

    @property
    def rows(self):
        return self.nb * self.seq_len

    @property
    def n_tiles(self):
        return (self.batch // self.nb) * self.n_tt

    @property
    def x_len(self):
        return self.valid_len if self.n_tt == 1 else self.seq_len

    @property
    def carry(self):
        return self.nb == 1


def _bf(x):
    return x.astype(BF16)


def _dot(a, b):
    return jnp.dot(a, b, preferred_element_type=F32)


def _dot_nt(a, b):
    return lax.dot_general(a, b, (((1,), (1,)), ((), ())), preferred_element_type=F32)


def _rms(x, g_row):
    return x * lax.rsqrt(jnp.mean(x * x, axis=-1, keepdims=True) + EPS) * g_row


def _softplus(x):
    return jnp.maximum(x, 0.0) + jnp.log(1.0 + jnp.exp(-jnp.abs(x)))


def _silu(x):
    return x * jax.nn.sigmoid(x)


def _log2(n):
    k = n.bit_length() - 1
    assert (1 << k) == n, n
    return k


def _halves(x):
    return x[:, :LANES], x[:, LANES:]


def _pair(a, b):
    return jnp.concatenate([a, b], axis=1)


def _block_diag(y):
    ya, yb = _halves(y)
    z = jnp.zeros_like(ya)
    return jnp.concatenate([_pair(ya, z), _pair(z, yb)], axis=0)


def _pmm(x, y):
    return _dot(_bf(x), _block_diag(_bf(y)))


def _each(f, *lists):
    return [f(*xs) for xs in zip(*lists)]


def _problems(cfg):
    c = cfg.chunk
    if cfg.carry:
        assert 2 * c == VC_ROWS and GDN_HEADS == 4
        n = -(-cfg.valid_len // c)
        return [([(u * c, c, 0), (u * c, c, 2)], [(u * c, c, 1), (u * c, c, 3)]) for u in range(n)]
    half = VC_ROWS // GDN_HEADS
    assert half % cfg.seq_len == 0 and cfg.seq_len == c and cfg.rows % (2 * half) == 0
    return [([(u * 2 * half, half, h) for h in range(GDN_HEADS)],
             [(u * 2 * half + half, half, h) for h in range(GDN_HEADS)]) for u in range(cfg.rows // (2 * half))]


@dataclasses.dataclass
class Prep:
    u: jax.Array
    w: jax.Array
    qk: jax.Array
    q_dec: jax.Array
    k_dec: jax.Array
    g_last: tuple


def _delta_prep(cfg, problems, qkv_scr, gb_scr, gc_scr, out):
    c = cfg.chunk
    lc = _log2(c)
    ii = lax.broadcasted_iota(jnp.int32, (VC_ROWS, 2 * LANES), 0)
    jj = lax.broadcasted_iota(jnp.int32, (VC_ROWS, 2 * LANES), 1) & (LANES - 1)
    same = (ii >> lc) == (jj >> lc)
    lower = same & (jj <= ii)
    strict = same & (jj < ii)
    end_col = (jj == (ii | (c - 1)))[:, :LANES]
    eye = jnp.where(ii == jj, 1.0, 0.0)

    def stacked(ref, col, width, half):
        return jnp.concatenate([ref[r0:r0 + n, col(h):col(h) + width] for (r0, n, h) in half], axis=0)

    def packed(ref, base):
        return [_pair(*[stacked(ref, lambda h: base + LANES * h, LANES, half) for half in pr]) for pr in problems]

    def columns(ref, base):
        return [[stacked(ref, lambda h: base + h, 1, half) for half in pr] for pr in problems]

    def spread(cols):
        return _pair(*[jnp.broadcast_to(col, (VC_ROWS, LANES)) for col in cols])

    q, k, v = packed(qkv_scr, Q_COL), packed(qkv_scr, K_COL), packed(qkv_scr, V_COL)
    beta_c, gcum_c = columns(gb_scr, GDN_HEADS), columns(gc_scr, 0)
    beta, gcum = _each(spread, beta_c), _each(spread, gcum_c)
    g_rows = _each(lambda g: _pair(*[h.T for h in _halves(g)]), gcum)
    decay = _each(lambda g, gr: jnp.where(lower, jnp.exp(g - gr), 0.0), gcum, g_rows)
    g_last = _each(lambda gr: tuple(jnp.sum(jnp.where(end_col, h, 0.0), axis=1, keepdims=True) for h in _halves(gr)),
                   g_rows)
    kq = _each(lambda kk, qq: _dot_nt(_bf(jnp.concatenate([kk, qq], axis=0)), _block_diag(_bf(kk))), k, q)
    qk = _each(lambda m, d: m[VC_ROWS:] * d, kq, decay)
    a = _each(lambda m, b, d: jnp.where(strict, m[:VC_ROWS] * b * d, 0.0), kq, beta, decay)
    yield

    base = min(c, NEUMANN_BASE)
    lb = _log2(base)
    p = _each(lambda x: jnp.where((ii >> lb) == (jj >> lb), x, 0.0), a)
    t = _each(lambda x: eye - x, p)
    for _ in range(lb - 1):
        p = _each(_pmm, p, p)
        t = _each(lambda x, y: x + _pmm(x, y), t, p)
        yield
    size = base
    while size < c:
        ls = _log2(size)
        couple = ((ii >> (ls + 1)) == (jj >> (ls + 1))) & ((ii >> ls) != (jj >> ls))
        off = _each(lambda x: jnp.where(couple, x, 0.0), a)
        t_off = _each(_pmm, t, off)
        t = _each(lambda x, y: x - _pmm(y, x), t, t_off)
        size *= 2
        yield

    exp_g = _each(lambda cols: spread([jnp.exp(col) for col in cols]), gcum_c)
    k_scale = _each(lambda cols, gl: spread([jnp.exp(e - col) for col, e in zip(cols, gl)]), gcum_c, g_last)
    u = _each(lambda x, vv, b: _pmm(x, vv * b), t, v, beta)
    w = _each(lambda x, kk, b, e: _pmm(x, kk * (b * e)), t, k, beta, exp_g)
    q_dec = _each(lambda x, e: x * e, q, exp_g)
    k_dec = _each(lambda x, s: x * s, k, k_scale)
    out["preps"] = [Prep(*xs) for xs in zip(u, w, qk, q_dec, k_dec, g_last)]


def _delta_apply_carry(cfg, problems, preps, s_scr, o_scr):
    c = cfg.chunk
    top_rows = lax.broadcasted_iota(jnp.int32, (VC_ROWS, GDN_DV), 0) < c
    for (half_a, half_b), pr in zip(problems, preps):
        state = [s_scr[h] for h in range(GDN_HEADS)]
        w_s, q_s = [], []
        for blk in range(VC_ROWS // c):
            lo = blk * c
            wq = _bf(jnp.concatenate([pr.w[lo:lo + c], pr.q_dec[lo:lo + c]], axis=0))
            r = _dot(wq, _block_diag(_bf(_pair(state[half_a[blk][2]], state[half_b[blk][2]]))))
            w_s.append(r[:c])
            q_s.append(r[c:])
        v_new = pr.u - jnp.concatenate(w_s, axis=0)
        o = jnp.concatenate(q_s, axis=0) + _pmm(pr.qk, v_new)
        for half, vn, kd, gl, o_h in zip((half_a, half_b), _halves(v_new), _halves(pr.k_dec), pr.g_last, _halves(o)):
            upd = _dot(_bf(kd.T), _bf(_pair(jnp.where(top_rows, vn, 0.0), jnp.where(top_rows, 0.0, vn))))
            for blk, (r0, n, h) in enumerate(half):
                lo = blk * c
                s_scr[h] = state[h] * jnp.exp(gl[lo:lo + 1, :]) + upd[:, GDN_DV * blk:GDN_DV * (blk + 1)]
                o_scr[r0:r0 + n, GDN_DV * h:GDN_DV * (h + 1)] = o_h[lo:lo + c]


def _delta_apply_blocks(cfg, problems, preps, ssm_ref, nssm_ref, o_scr):
    c = cfg.chunk
    lc = _log2(c)
    blk_id = lax.broadcasted_iota(jnp.int32, (VC_ROWS, GDN_DV), 0) >> lc
    for pieces, pr in zip(problems, preps):
        halves = []
        for half, w_h, qd_h, u_h in zip(pieces, _halves(pr.w), _halves(pr.q_dec), _halves(pr.u)):
            blocks = [(r0 + t0, h, (r0 + t0) // cfg.seq_len) for (r0, n, h) in half for t0 in range(0, n, c)]
            vs, qs = [], []
            for b, (_, h, seq) in enumerate(blocks):
                lo = b * c
                r = _dot(_bf(jnp.concatenate([w_h[lo:lo + c], qd_h[lo:lo + c]], axis=0)), _bf(ssm_ref[seq, h]))
                vs.append(u_h[lo:lo + c] - r[:c])
                qs.append(r[c:])
            halves.append((blocks, jnp.concatenate(vs, axis=0), jnp.concatenate(qs, axis=0)))
            yield
        v_new = _pair(halves[0][1], halves[1][1])
        o = _pair(halves[0][2], halves[1][2]) + _pmm(pr.qk, v_new)
        for (blocks, vn, _), kd, gl, o_h in zip(halves, _halves(pr.k_dec), pr.g_last, _halves(o)):
            kd_t = _bf(kd.T)
            for b0 in range(0, len(blocks), 2):
                upd = _dot(kd_t, _bf(_pair(jnp.where(blk_id == b0, vn, 0.0), jnp.where(blk_id == b0 + 1, vn, 0.0))))
                for b in (b0, b0 + 1):
                    row, h, seq = blocks[b]
                    lo = b * c
                    nssm_ref[seq, h] = (ssm_ref[seq, h] * jnp.exp(gl[lo:lo + 1, :])
                                        + upd[:, GDN_DV * (b - b0):GDN_DV * (b - b0 + 1)])
                    o_scr[row:row + c, GDN_DV * h:GDN_DV * (h + 1)] = o_h[lo:lo + c]
                if b0 % 8 == 6:
                    yield


def _layer_kernel(cfg, x_ref, xb_ref, ta_ref, tq_ref, ssm_ref, nmix_ref, win_ref, caw_ref, cqw_ref, alog_ref, dt_ref,
                  gnorm_ref, wout_ref, nffn_ref, wup_ref, wdown_ref, nfin_ref,
                  y_ref, nta_ref, ntq_ref, nssm_ref,
                  z_scr, xa_buf, xq_buf, qkv_scr, gb_scr, gc_scr, o_scr, s_scr, mix_scr, up_scr):
    step = pl.program_id(0)
    nb, sl, rows = cfg.nb, cfg.seq_len, cfg.rows
    if cfg.pipelined:
        tile = jnp.minimum(step, cfg.n_tiles - 1)
        real = step < cfg.n_tiles
    else:
        tile = step
        real = True
    j = lax.rem(tile, cfg.n_tt)
    first = j == 0
    last = (j == cfg.n_tt - 1) & real

    @pl.when(first)
    def _():
        xa_buf[...] = jnp.zeros(xa_buf.shape, F32)
        xq_buf[...] = jnp.zeros(xq_buf.shape, F32)
        xa_buf[:, SUBLANES - (SCONV_K - 1):, :] = ta_ref[...]
        xq_buf[:, SUBLANES - (GDN_CONV_K - 1):, :] = tq_ref[...]
        if cfg.carry:
            s_scr[...] = ssm_ref[0]

    st = {}
    problems = _problems(cfg)

    def padded_rows(ref):
        x = ref[...]
        if cfg.x_len < sl:
            x = jnp.concatenate([x, jnp.zeros((nb, sl - cfg.x_len, D_MODEL), F32)], axis=1)
        return x.reshape(rows, D_MODEL)

    def proj(lo, hi):
        z_scr[:, lo:hi] = _dot(st["xn"], win_ref[:, lo:hi])

    if cfg.pipelined:
        @pl.when(step == 0)
        def _():
            mix_scr[...] = jnp.zeros(mix_scr.shape, mix_scr.dtype)

    tail_lo = (cfg.valid_len - 1) // SUBLANES * SUBLANES
    assert cfg.valid_len - tail_lo >= GDN_CONV_K - 1 and (cfg.n_tt == 1 or cfg.valid_len == sl)

    def causal_conv(x, prev_buf, w_ref, col, width):
        taps = w_ref.shape[0]
        t8 = lax.broadcasted_iota(jnp.int32, (nb, SUBLANES, width), 1)
        prev = prev_buf[:, :, col:col + width]
        acc = x * w_ref[taps - 1:taps, col:col + width]
        for back in range(1, taps):
            moved = pltpu.roll(x, back, 1)
            head = jnp.where(t8 < back, pltpu.roll(prev, back, 1), moved[:, :SUBLANES, :])
            moved = head if sl == SUBLANES else jnp.concatenate([head, moved[:, SUBLANES:, :]], axis=1)
            acc = acc + moved * w_ref[taps - 1 - back:taps - back, col:col + width]
        prev_buf[:, :, col:col + width] = x[:, tail_lo:tail_lo + SUBLANES, :]
        return acc

    def conv_q(col):
        width = GDN_HEADS * GDN_DK
        x = z_scr[:, O_QKV + col:O_QKV + col + width].reshape(nb, sl, width)
        act = _silu(causal_conv(x, xq_buf, cqw_ref, col, width).reshape(rows, width))
        if col == V_COL:
            qkv_scr[:, col:col + width] = act
            return
        scale = GDN_DK ** -0.5 if col == Q_COL else 1.0
        for h in range(GDN_HEADS):
            ah = act[:, GDN_DK * h:GDN_DK * (h + 1)]
            qkv_scr[:, col + GDN_DK * h:col + GDN_DK * (h + 1)] = (
                ah * (lax.rsqrt(jnp.sum(ah * ah, axis=-1, keepdims=True) + EPS) * scale))

    def conv_a():
        x = (z_scr[:, O_C:O_H] * z_scr[:, O_H:O_QKV]).reshape(nb, sl, CONV_CH)
        acc = causal_conv(x, xa_buf, caw_ref, 0, CONV_CH)
        mix_scr[:, :CONV_CH] = _bf(z_scr[:, O_B:O_C] * acc.reshape(rows, CONV_CH))

    def gates():
        zab = z_scr[:, O_AB:IN_W_PAD]
        g = -jnp.exp(alog_ref[...]) * _softplus(zab + dt_ref[...])
        beta = jax.nn.sigmoid(zab)
        lane = lax.broadcasted_iota(jnp.int32, (rows, LANES), 1)
        gb = jnp.where(lane < GDN_HEADS, g, beta)
        if cfg.valid_len < sl:
            t_in_seq = lax.broadcasted_iota(jnp.int32, (rows, LANES), 0) & (sl - 1)
            gb = jnp.where(t_in_seq < cfg.valid_len, gb, 0.0)
        gb_scr[...] = gb
        ri = lax.broadcasted_iota(jnp.int32, (rows, rows), 0)
        rj = lax.broadcasted_iota(jnp.int32, (rows, rows), 1)
        lc = _log2(cfg.chunk)
        tri = jnp.where(((ri >> lc) == (rj >> lc)) & (rj <= ri), 1.0, 0.0).astype(BF16)
        g1 = _bf(gb)
        r1 = gb - g1.astype(F32)
        g2 = _bf(r1)
        g3 = _bf(r1 - g2.astype(F32))
        gc_scr[...] = _dot(tri, g1) + (_dot(tri, g2) + _dot(tri, g3))

    def gated_norm(lo, hi):
        for h in range(GDN_HEADS):
            oh = o_scr[lo:hi, GDN_DV * h:GDN_DV * (h + 1)]
            gate = z_scr[lo:hi, O_G + GDN_DV * h:O_G + GDN_DV * (h + 1)]
            mix_scr[lo:hi, CONV_CH + GDN_DV * h:CONV_CH + GDN_DV * (h + 1)] = _bf(
                _rms(oh, gnorm_ref[...]) * _silu(gate))

    def project(src_ref, groups):
        if "xn" not in st:
            st["xn"] = _bf(_rms(padded_rows(src_ref), nmix_ref[...]))
        for group in groups:
            if group == 0:
                proj(O_QKV, O_G)
            elif group == 1:
                proj(O_B, O_QKV)
            else:
                proj(O_G, O_AB)
                z_ab = _dot(st["xn"], win_ref[:, O_AB:O_AB + 2 * GDN_HEADS])
                z_scr[:, O_AB:] = jnp.pad(z_ab, ((0, 0), (0, LANES - 2 * GDN_HEADS)))

    def front_half():
        project(x_ref, (0, 1))
        yield
        conv_q(Q_COL)
        yield
        project(x_ref, (2,))
        yield
        conv_q(K_COL)
        yield
        conv_q(V_COL)
        yield
        conv_a()
        yield
        gates()
        yield
        for _ in _delta_prep(cfg, problems, qkv_scr, gb_scr, gc_scr, st):
            yield
        per = cfg.chunk if cfg.carry else 2 * (VC_ROWS // GDN_HEADS)
        done = len(problems) * per
        if done < rows:
            o_scr[done:, :] = jnp.zeros((rows - done, GDN_HEADS * GDN_DV), F32)
            gated_norm(done, rows)
        yield
        for i in range(len(problems)):
            if cfg.carry:
                _delta_apply_carry(cfg, problems[i:i + 1], st["preps"][i:i + 1], s_scr, o_scr)
            else:
                yield from _delta_apply_blocks(cfg, problems[i:i + 1], st["preps"][i:i + 1], ssm_ref, nssm_ref, o_scr)
            gated_norm(i * per, (i + 1) * per)
            yield

    def back_half():
        h1 = padded_rows(xb_ref) + _dot(mix_scr[...], wout_ref[...])
        hn = _bf(_rms(h1, nffn_ref[...]))
        yield
        for c in range(FFN_HIDDEN // FFN_CHUNK):
            up = jnp.maximum(_dot(hn, wup_ref[:, c * FFN_CHUNK:(c + 1) * FFN_CHUNK]), 0.0)
            up_scr[:, c * FFN_CHUNK:(c + 1) * FFN_CHUNK] = _bf(up * up)
            yield
        outs = []
        for n in range(D_MODEL // FFN_CHUNK):
            cols = slice(n * FFN_CHUNK, (n + 1) * FFN_CHUNK)
            outs.append(h1[:, cols] + _dot(up_scr[...], wdown_ref[:, cols]))
            yield
        h2 = jnp.concatenate(outs, axis=1)
        y_ref[...] = _rms(h2, nfin_ref[...]).reshape(nb, sl, D_MODEL)[:, :cfg.x_len, :]

    back = back_half()
    if cfg.states_only:
        back = iter(())
        y_ref[...] = jnp.zeros(y_ref.shape, F32)
    if cfg.pipelined:
        next(back)
        for i, _ in enumerate(front_half()):
            for _ in range(cfg.back_after[i] if i < len(cfg.back_after) else 0):
                next(back, None)
    else:
        for _ in front_half():
            pass
    for _ in back:
        pass

    @pl.when(last)
    def _():
        end = cfg.valid_len - tail_lo
        nta_ref[...] = xa_buf[:, end - (SCONV_K - 1):end, :]
        ntq_ref[...] = xq_buf[:, end - (GDN_CONV_K - 1):end, :]
        if cfg.carry:
            nssm_ref[0] = s_scr[...]


def _layer_call(cfg, name, x, tail_a, tail_q, ssm, params):
    nb, sl, rows = cfg.nb, cfg.seq_len, cfg.rows
    assert x.shape == (cfg.batch, cfg.x_len * cfg.n_tt, D_MODEL), (x.shape, cfg)
    assert cfg.batch % nb == 0 and sl % SUBLANES == 0 and rows % VC_ROWS == 0
    assert not cfg.shared_state or nb == 1
    n_tiles, n_tt = cfg.n_tiles, cfg.n_tt

    def front_tile(s):
        return jnp.minimum(s, n_tiles - 1) if cfg.pipelined else s

    def back_tile(s):
        return jnp.maximum(s - 1, 0) if cfg.pipelined else s

    def tile_spec(which):
        return pl.BlockSpec((nb, cfg.x_len, D_MODEL), lambda s: (which(s) // n_tt, which(s) % n_tt, 0))

    def state_spec(shape):
        zeros = (0,) * len(shape)
        if cfg.shared_state:
            return pl.BlockSpec((nb,) + shape, lambda s: (0,) + zeros)
        return pl.BlockSpec((nb,) + shape, lambda s: (front_tile(s) // n_tt,) + zeros)

    def out_state_spec(shape):
        zeros = (0,) * len(shape)
        return pl.BlockSpec((nb,) + shape, lambda s: (front_tile(s) // n_tt,) + zeros)

    def const_spec(arr):
        zeros = (0,) * arr.ndim
        return pl.BlockSpec(arr.shape, lambda s: zeros, pipeline_mode=pl.Buffered(1))

    ta_shape, tq_shape = (SCONV_K - 1, CONV_CH), (GDN_CONV_K - 1, QKV_W)
    ssm_shape = (GDN_HEADS, GDN_DK, GDN_DV)
    in_specs = [tile_spec(front_tile), tile_spec(back_tile),
                state_spec(ta_shape), state_spec(tq_shape), state_spec(ssm_shape)]
    in_specs += [pl.BlockSpec(memory_space=pl.ANY) if cfg.states_only and i >= FIRST_BACK_PARAM else const_spec(p)
                 for i, p in enumerate(params)]
    out_specs = [tile_spec(back_tile), out_state_spec(ta_shape), out_state_spec(tq_shape), out_state_spec(ssm_shape)]
    out_shape = [
        jax.ShapeDtypeStruct(x.shape, F32),
        jax.ShapeDtypeStruct((cfg.batch,) + ta_shape, F32),
        jax.ShapeDtypeStruct((cfg.batch,) + tq_shape, F32),
        jax.ShapeDtypeStruct((cfg.batch,) + ssm_shape, F32),
    ]
    scratch = [
        pltpu.VMEM((rows, IN_W_PAD), F32),
        pltpu.VMEM((nb, SUBLANES, CONV_CH), F32),
        pltpu.VMEM((nb, SUBLANES, QKV_W), F32),
        pltpu.VMEM((rows, QKV_W), F32),
        pltpu.VMEM((rows, LANES), F32),
        pltpu.VMEM((rows, LANES), F32),
        pltpu.VMEM((rows, GDN_HEADS * GDN_DV), F32),
        pltpu.VMEM(ssm_shape, F32),
        pltpu.VMEM((rows, D_MODEL), BF16),
        pltpu.VMEM((rows, FFN_HIDDEN), BF16),
    ]
    return pl.pallas_call(
        functools.partial(_layer_kernel, cfg),
        grid=(n_tiles + (1 if cfg.pipelined else 0),),
        in_specs=in_specs,
        out_specs=out_specs,
        out_shape=out_shape,
        scratch_shapes=scratch,
        compiler_params=pltpu.CompilerParams(
            dimension_semantics=("arbitrary",), vmem_limit_bytes=VMEM_LIMIT_BYTES),
        name=name,
    )(x, x, tail_a, tail_q, ssm, *params)


CAST_STEPS = 8


def _cast_kernel(*refs):
    n = len(refs) // 2
    for src, dst in zip(refs[:n], refs[n:]):
        dst[...] = src[...].astype(BF16)


def _to_bf16(*weights):
    def spec(w):
        assert w.ndim == 2 and w.shape[0] % (CAST_STEPS * 2 * SUBLANES) == 0, w.shape
        return pl.BlockSpec((w.shape[0] // CAST_STEPS, w.shape[1]), lambda i: (i, 0))

    return pl.pallas_call(
        _cast_kernel,
        grid=(CAST_STEPS,),
        in_specs=[spec(w) for w in weights],
        out_specs=[spec(w) for w in weights],
        out_shape=[jax.ShapeDtypeStruct(w.shape, BF16) for w in weights],
        compiler_params=pltpu.CompilerParams(
            dimension_semantics=("arbitrary",), vmem_limit_bytes=VMEM_LIMIT_BYTES),
        name="cast_weights",
    )(*weights)


META_PAD_LEN = 128
FIRST_BACK_PARAM = 7
PROMPT_TILE = 256
SAMPLE_PAD_LEN = 8
SAMPLE_SEQS_PER_STEP = 16


def kernel(x_prompt, x_sample, state_conv_a, state_conv_qkv, state_ssm, meta_tokens, norm_mix, w_in, conv_a_w,
           conv_qkv_w, a_log, dt_bias, gdn_norm, w_out, norm_ffn, w_up, w_down, norm_final):
    assert norm_mix.shape[0] == 1, "single layer"
    bp, tp, _ = x_prompt.shape
    bs, ts, _ = x_sample.shape
    n_meta = meta_tokens.shape[0]

    def row(v):
        return v.reshape(1, -1).astype(F32)

    def lane_row(v):
        return jnp.pad(v.astype(F32), (0, LANES - v.shape[0])).reshape(1, LANES)

    w_in_b, w_out_b, w_up_b, w_down_b = _to_bf16(w_in[0], w_out[0], w_up[0], w_down[0])
    params = (
        row(norm_mix[0]),
        w_in_b,
        conv_a_w[0].astype(F32), conv_qkv_w[0].astype(F32),
        lane_row(a_log[0]), lane_row(dt_bias[0]), row(gdn_norm[0]),
        w_out_b, row(norm_ffn[0]),
        w_up_b,
        w_down_b,
        row(norm_final),
    )

    meta_cfg = Cfg(batch=1, nb=1, seq_len=META_PAD_LEN, valid_len=n_meta, chunk=64, n_tt=1, shared_state=False,
                   pipelined=False, states_only=True)
    meta_x = meta_tokens.astype(F32)[None]
    _, ta_m, tq_m, ssm_m = _layer_call(
        meta_cfg, "layer_meta", meta_x,
        jnp.zeros((1, SCONV_K - 1, CONV_CH), F32), jnp.zeros((1, GDN_CONV_K - 1, QKV_W), F32),
        jnp.zeros((1, GDN_HEADS, GDN_DK, GDN_DV), F32), params)

    prompt_cfg = Cfg(batch=bp, nb=1, seq_len=PROMPT_TILE, valid_len=PROMPT_TILE, chunk=64,
                     n_tt=tp // PROMPT_TILE, shared_state=True, pipelined=True, states_only=False,
                     back_after=PROMPT_BACK_AFTER)
    y_p, ta_p, tq_p, ssm_p = _layer_call(prompt_cfg, "layer_prompt", x_prompt, ta_m, tq_m, ssm_m, params)

    sample_cfg = Cfg(batch=bs, nb=SAMPLE_SEQS_PER_STEP, seq_len=SAMPLE_PAD_LEN, valid_len=ts, chunk=SAMPLE_PAD_LEN,
                     n_tt=1, shared_state=False, pipelined=False, states_only=False)
    y_s, ta_s, tq_s, ssm_s = _layer_call(
        sample_cfg, "layer_sample", x_sample, state_conv_a[0], state_conv_qkv[0], state_ssm[0], params)

    return (y_p, y_s, ta_p[None], tq_p[None], ssm_p[None], ta_s[None], tq_s[None], ssm_s[None])
```

```python
import dataclasses
import functools

import jax
import jax.numpy as jnp
from jax import lax
from jax.experimental import pallas as pl
from jax.experimental.pallas import tpu as pltpu

F32 = jnp.float32
BF16 = jnp.bfloat16

D_MODEL = 1024
CONV_CH = 512
SCONV_K = 3
GDN_HEADS = 4
GDN_DK = 128
GDN_DV = 128
GDN_CONV_K = 4
QKV_W = GDN_HEADS * (2 * GDN_DK + GDN_DV)
FFN_HIDDEN = 4 * D_MODEL
EPS = 1e-6

O_B = 0
O_C = O_B + CONV_CH
O_H = O_C + CONV_CH
O_QKV = O_H + CONV_CH
O_G = O_QKV + QKV_W
O_AB = O_G + GDN_HEADS * GDN_DV
LANES = 128
SUBLANES = 8
IN_W_PAD = O_AB + LANES
VC_ROWS = 128
NEUMANN_BASE = 16
FFN_CHUNK = 256
PREP_GROUP = 4
PROMPT_BACK_AFTER = (0, 0, 0, 3, 2, 3, 2, 2, 1, 1, 1, 1, 1, 0, 1, 1, 1, 1)
VMEM_LIMIT_BYTES = 56 * 1024 * 1024

Q_COL = 0
K_COL = GDN_HEADS * GDN_DK
V_COL = 2 * GDN_HEADS * GDN_DK


@dataclasses.dataclass(frozen=True)
class Cfg:
    batch: int
    nb: int
    seq_len: int
    valid_len: int
    chunk: int
    n_tt: int
    shared_state: bool
    pipelined: bool
    states_only: bool
    back_after: tuple = ()

    @property
    def rows(self):
        return self.nb * self.seq_len

    @property
    def n_tiles(self):
        return (self.batch // self.nb) * self.n_tt

    @property
    def x_len(self):
        return self.valid_len if self.n_tt == 1 else self.seq_len

    @property
    def carry(self):
        return self.nb == 1


def _bf(x):
    return x.astype(BF16)


def _dot(a, b):
    return jnp.dot(a, b, preferred_element_type=F32)


def _dot_nt(a, b):
    return lax.dot_general(a, b, (((1,), (1,)), ((), ())), preferred_element_type=F32)


def _rms(x, g_row):
    return x * lax.rsqrt(jnp.mean(x * x, axis=-1, keepdims=True) + EPS) * g_row


def _softplus(x):
    return jnp.maximum(x, 0.0) + jnp.log(1.0 + jnp.exp(-jnp.abs(x)))


def _silu(x):
    return x * jax.nn.sigmoid(x)


def _log2(n):
    k = n.bit_length() - 1
    assert (1 << k) == n, n
    return k


def _halves(x):
    return x[:, :LANES], x[:, LANES:]


def _pair(a, b):
    return jnp.concatenate([a, b], axis=1)


def _block_diag(y):
    ya, yb = _halves(y)
    z = jnp.zeros_like(ya)
    return jnp.concatenate([_pair(ya, z), _pair(z, yb)], axis=0)


def _pmm(x, y):
    return _dot(_bf(x), _block_diag(_bf(y)))


def _each(f, *lists):
    return [f(*xs) for xs in zip(*lists)]


def _problems(cfg):
    c = cfg.chunk
    if cfg.carry:
        assert 2 * c == VC_ROWS and GDN_HEADS == 4
        n = -(-cfg.valid_len // c)
        return [([(u * c, c, 0), (u * c, c, 2)], [(u * c, c, 1), (u * c, c, 3)]) for u in range(n)]
    half = VC_ROWS // GDN_HEADS
    assert half % cfg.seq_len == 0 and cfg.seq_len == c and cfg.rows % (2 * half) == 0
    return [([(u * 2 * half, half, h) for h in range(GDN_HEADS)],
             [(u * 2 * half + half, half, h) for h in range(GDN_HEADS)]) for u in range(cfg.rows // (2 * half))]


@dataclasses.dataclass
class Prep:
    u: jax.Array
    w: jax.Array
    qk: jax.Array
    q_dec: jax.Array
    k_dec: jax.Array
    g_last: tuple


def _delta_prep(cfg, problems, qkv_scr, gb_scr, gc_scr, out):
    c = cfg.chunk
    lc = _log2(c)
    ii = lax.broadcasted_iota(jnp.int32, (VC_ROWS, 2 * LANES), 0)
    jj = lax.broadcasted_iota(jnp.int32, (VC_ROWS, 2 * LANES), 1) & (LANES - 1)
    same = (ii >> lc) == (jj >> lc)
    lower = same & (jj <= ii)
    strict = same & (jj < ii)
    end_col = (jj == (ii | (c - 1)))[:, :LANES]
    eye = jnp.where(ii == jj, 1.0, 0.0)

    def stacked(ref, col, width, half):
        return jnp.concatenate([ref[r0:r0 + n, col(h):col(h) + width] for (r0, n, h) in half], axis=0)

    def packed(ref, base):
        return [_pair(*[stacked(ref, lambda h: base + LANES * h, LANES, half) for half in pr]) for pr in problems]

    def columns(ref, base):
        return [[stacked(ref, lambda h: base + h, 1, half) for half in pr] for pr in problems]

    def spread(cols):
        return _pair(*[jnp.broadcast_to(col, (VC_ROWS, LANES)) for col in cols])

    q, k, v = packed(qkv_scr, Q_COL), packed(qkv_scr, K_COL), packed(qkv_scr, V_COL)
    beta_c, gcum_c = columns(gb_scr, GDN_HEADS), columns(gc_scr, 0)
    beta, gcum = _each(spread, beta_c), _each(spread, gcum_c)
    g_rows = _each(lambda g: _pair(*[h.T for h in _halves(g)]), gcum)
    decay = _each(lambda g, gr: jnp.where(lower, jnp.exp(g - gr), 0.0), gcum, g_rows)
    g_last = _each(lambda gr: tuple(jnp.sum(jnp.where(end_col, h, 0.0), axis=1, keepdims=True) for h in _halves(gr)),
                   g_rows)
    kq = _each(lambda kk, qq: _dot_nt(_bf(jnp.concatenate([kk, qq], axis=0)), _block_diag(_bf(kk))), k, q)
    qk = _each(lambda m, d: m[VC_ROWS:] * d, kq, decay)
    a = _each(lambda m, b, d: jnp.where(strict, m[:VC_ROWS] * b * d, 0.0), kq, beta, decay)
    yield

    folded = 2 * c == VC_ROWS
    if folded:
        ii = lax.broadcasted_iota(jnp.int32, (c, 2 * LANES), 0)
        lane = lax.broadcasted_iota(jnp.int32, (c, 2 * LANES), 1)
        jj, blk_of_lane = lane & (c - 1), lane >> lc
        eye = jnp.where(ii == jj, 1.0, 0.0)
        a = _each(lambda x: x[:c] + x[c:], a)
        keep = [_bf(jnp.where(blk_of_lane == b, 1.0, 0.0)) for b in range(2 * LANES // c)]

        def mm(x, y):
            yb = _bf(y)
            return _dot(_bf(x), jnp.concatenate([yb * m for m in keep], axis=0))
    else:
        mm = _pmm
    base = min(c, NEUMANN_BASE)
    lb = _log2(base)
    p = _each(lambda x: jnp.where((ii >> lb) == (jj >> lb), x, 0.0), a)
    t = _each(lambda x: eye - x, p)
    for _ in range(lb - 1):
        p = _each(mm, p, p)
        t = _each(lambda x, y: x + mm(x, y), t, p)
        yield
    size = base
    while size < c:
        ls = _log2(size)
        couple = ((ii >> (ls + 1)) == (jj >> (ls + 1))) & ((ii >> ls) != (jj >> ls))
        off = _each(lambda x: jnp.where(couple, x, 0.0), a)
        t_off = _each(mm, t, off)
        t = _each(lambda x, y: x - mm(y, x), t, t_off)
        size *= 2
        yield
    if folded:
        first_blk = (blk_of_lane & 1) == 0
        t = _each(lambda x: jnp.concatenate([jnp.where(first_blk, x, 0.0), jnp.where(first_blk, 0.0, x)], axis=0), t)

    exp_g = _each(lambda cols: spread([jnp.exp(col) for col in cols]), gcum_c)
    k_scale = _each(lambda cols, gl: spread([jnp.exp(e - col) for col, e in zip(cols, gl)]), gcum_c, g_last)
    u = _each(lambda x, vv, b: _pmm(x, vv * b), t, v, beta)
    w = _each(lambda x, kk, b, e: _pmm(x, kk * (b * e)), t, k, beta, exp_g)
    q_dec = _each(lambda x, e: x * e, q, exp_g)
    k_dec = _each(lambda x, s: x * s, k, k_scale)
    out.setdefault("preps", []).extend(Prep(*xs) for xs in zip(u, w, qk, q_dec, k_dec, g_last))


def _delta_apply_carry(cfg, problems, preps, s_scr, o_scr):
    c = cfg.chunk
    top_rows = lax.broadcasted_iota(jnp.int32, (VC_ROWS, GDN_DV), 0) < c
    for (half_a, half_b), pr in zip(problems, preps):
        state = [s_scr[h] for h in range(GDN_HEADS)]
        w_s, q_s = [], []
        for blk in range(VC_ROWS // c):
            lo = blk * c
            wq = _bf(jnp.concatenate([pr.w[lo:lo + c], pr.q_dec[lo:lo + c]], axis=0))
            r = _dot(wq, _block_diag(_bf(_pair(state[half_a[blk][2]], state[half_b[blk][2]]))))
            w_s.append(r[:c])
            q_s.append(r[c:])
        v_new = pr.u - jnp.concatenate(w_s, axis=0)
        o = jnp.concatenate(q_s, axis=0) + _pmm(pr.qk, v_new)
        for half, vn, kd, gl, o_h in zip((half_a, half_b), _halves(v_new), _halves(pr.k_dec), pr.g_last, _halves(o)):
            upd = _dot(_bf(kd.T), _bf(_pair(jnp.where(top_rows, vn, 0.0), jnp.where(top_rows, 0.0, vn))))
            for blk, (r0, n, h) in enumerate(half):
                lo = blk * c
                s_scr[h] = state[h] * jnp.exp(gl[lo:lo + 1, :]) + upd[:, GDN_DV * blk:GDN_DV * (blk + 1)]
                o_scr[r0:r0 + n, GDN_DV * h:GDN_DV * (h + 1)] = o_h[lo:lo + c]


def _delta_apply_blocks(cfg, problems, preps, ssm_ref, nssm_ref, o_scr):
    c = cfg.chunk
    lc = _log2(c)
    blk_id = lax.broadcasted_iota(jnp.int32, (VC_ROWS, GDN_DV), 0) >> lc
    for pieces, pr in zip(problems, preps):
        halves = []
        for half, w_h, qd_h, u_h in zip(pieces, _halves(pr.w), _halves(pr.q_dec), _halves(pr.u)):
            blocks = [(r0 + t0, h, (r0 + t0) // cfg.seq_len) for (r0, n, h) in half for t0 in range(0, n, c)]
            vs, qs = [], []
            for b, (_, h, seq) in enumerate(blocks):
                lo = b * c
                r = _dot(_bf(jnp.concatenate([w_h[lo:lo + c], qd_h[lo:lo + c]], axis=0)), _bf(ssm_ref[seq, h]))
                vs.append(u_h[lo:lo + c] - r[:c])
                qs.append(r[c:])
            halves.append((blocks, jnp.concatenate(vs, axis=0), jnp.concatenate(qs, axis=0)))
            yield
        v_new = _pair(halves[0][1], halves[1][1])
        o = _pair(halves[0][2], halves[1][2]) + _pmm(pr.qk, v_new)
        for (blocks, vn, _), kd, gl, o_h in zip(halves, _halves(pr.k_dec), pr.g_last, _halves(o)):
            kd_t = _bf(kd.T)
            for b0 in range(0, len(blocks), 2):
                upd = _dot(kd_t, _bf(_pair(jnp.where(blk_id == b0, vn, 0.0), jnp.where(blk_id == b0 + 1, vn, 0.0))))
                for b in (b0, b0 + 1):
                    row, h, seq = blocks[b]
                    lo = b * c
                    nssm_ref[seq, h] = (ssm_ref[seq, h] * jnp.exp(gl[lo:lo + 1, :])
                                        + upd[:, GDN_DV * (b - b0):GDN_DV * (b - b0 + 1)])
                    o_scr[row:row + c, GDN_DV * h:GDN_DV * (h + 1)] = o_h[lo:lo + c]
                if b0 % 8 == 6:
                    yield


def _layer_kernel(cfg, x_ref, xb_ref, ta_ref, tq_ref, ssm_ref, nmix_ref, win_ref, caw_ref, cqw_ref, alog_ref, dt_ref,
                  gnorm_ref, wout_ref, nffn_ref, wup_ref, wdown_ref, nfin_ref,
                  y_ref, nta_ref, ntq_ref, nssm_ref,
                  z_scr, xa_buf, xq_buf, qkv_scr, gb_scr, gc_scr, o_scr, s_scr, mix_scr, up_scr):
    step = pl.program_id(0)
    nb, sl, rows = cfg.nb, cfg.seq_len, cfg.rows
    if cfg.pipelined:
        tile = jnp.minimum(step, cfg.n_tiles - 1)
        real = step < cfg.n_tiles
    else:
        tile = step
        real = True
    j = lax.rem(tile, cfg.n_tt)
    first = j == 0
    last = (j == cfg.n_tt - 1) & real

    @pl.when(first)
    def _():
        xa_buf[...] = jnp.zeros(xa_buf.shape, F32)
        xq_buf[...] = jnp.zeros(xq_buf.shape, F32)
        xa_buf[:, SUBLANES - (SCONV_K - 1):, :] = ta_ref[...]
        xq_buf[:, SUBLANES - (GDN_CONV_K - 1):, :] = tq_ref[...]
        if cfg.carry:
            s_scr[...] = ssm_ref[0]

    st = {}
    problems = _problems(cfg)

    def padded_rows(ref):
        x = ref[...]
        if cfg.x_len < sl:
            x = jnp.concatenate([x, jnp.zeros((nb, sl - cfg.x_len, D_MODEL), F32)], axis=1)
        return x.reshape(rows, D_MODEL)

    def proj(lo, hi):
        z_scr[:, lo:hi] = _dot(st["xn"], win_ref[:, lo:hi])

    if cfg.pipelined:
        @pl.when(step == 0)
        def _():
            mix_scr[...] = jnp.zeros(mix_scr.shape, mix_scr.dtype)

    tail_lo = (cfg.valid_len - 1) // SUBLANES * SUBLANES
    assert cfg.valid_len - tail_lo >= GDN_CONV_K - 1 and (cfg.n_tt == 1 or cfg.valid_len == sl)

    def causal_conv(x, prev_buf, w_ref, col, width):
        taps = w_ref.shape[0]
        t8 = lax.broadcasted_iota(jnp.int32, (nb, SUBLANES, width), 1)
        prev = prev_buf[:, :, col:col + width]
        acc = x * w_ref[taps - 1:taps, col:col + width]
        for back in range(1, taps):
            moved = pltpu.roll(x, back, 1)
            head = jnp.where(t8 < back, pltpu.roll(prev, back, 1), moved[:, :SUBLANES, :])
            moved = head if sl == SUBLANES else jnp.concatenate([head, moved[:, SUBLANES:, :]], axis=1)
            acc = acc + moved * w_ref[taps - 1 - back:taps - back, col:col + width]
        prev_buf[:, :, col:col + width] = x[:, tail_lo:tail_lo + SUBLANES, :]
        return acc

    def conv_q(col):
        width = GDN_HEADS * GDN_DK
        x = z_scr[:, O_QKV + col:O_QKV + col + width].reshape(nb, sl, width)
        act = _silu(causal_conv(x, xq_buf, cqw_ref, col, width).reshape(rows, width))
        if col == V_COL:
            qkv_scr[:, col:col + width] = act
            return
        scale = GDN_DK ** -0.5 if col == Q_COL else 1.0
        for h in range(GDN_HEADS):
            ah = act[:, GDN_DK * h:GDN_DK * (h + 1)]
            qkv_scr[:, col + GDN_DK * h:col + GDN_DK * (h + 1)] = (
                ah * (lax.rsqrt(jnp.sum(ah * ah, axis=-1, keepdims=True) + EPS) * scale))

    def conv_a():
        x = (z_scr[:, O_C:O_H] * z_scr[:, O_H:O_QKV]).reshape(nb, sl, CONV_CH)
        acc = causal_conv(x, xa_buf, caw_ref, 0, CONV_CH)
        mix_scr[:, :CONV_CH] = _bf(z_scr[:, O_B:O_C] * acc.reshape(rows, CONV_CH))

    def gates():
        zab = z_scr[:, O_AB:IN_W_PAD]
        g = -jnp.exp(alog_ref[...]) * _softplus(zab + dt_ref[...])
        beta = jax.nn.sigmoid(zab)
        lane = lax.broadcasted_iota(jnp.int32, (rows, LANES), 1)
        gb = jnp.where(lane < GDN_HEADS, g, beta)
        if cfg.valid_len < sl:
            t_in_seq = lax.broadcasted_iota(jnp.int32, (rows, LANES), 0) & (sl - 1)
            gb = jnp.where(t_in_seq < cfg.valid_len, gb, 0.0)
        gb_scr[...] = gb
        ri = lax.broadcasted_iota(jnp.int32, (rows, rows), 0)
        rj = lax.broadcasted_iota(jnp.int32, (rows, rows), 1)
        lc = _log2(cfg.chunk)
        tri = jnp.where(((ri >> lc) == (rj >> lc)) & (rj <= ri), 1.0, 0.0).astype(BF16)
        g1 = _bf(gb)
        r1 = gb - g1.astype(F32)
        g2 = _bf(r1)
        g3 = _bf(r1 - g2.astype(F32))
        gc_scr[...] = _dot(tri, g1) + (_dot(tri, g2) + _dot(tri, g3))

    def gated_norm(lo, hi):
        for h in range(GDN_HEADS):
            oh = o_scr[lo:hi, GDN_DV * h:GDN_DV * (h + 1)]
            gate = z_scr[lo:hi, O_G + GDN_DV * h:O_G + GDN_DV * (h + 1)]
            mix_scr[lo:hi, CONV_CH + GDN_DV * h:CONV_CH + GDN_DV * (h + 1)] = _bf(
                _rms(oh, gnorm_ref[...]) * _silu(gate))

    def project(src_ref, groups):
        if "xn" not in st:
            st["xn"] = _bf(_rms(padded_rows(src_ref), nmix_ref[...]))
        for group in groups:
            if group == 0:
                proj(O_QKV, O_G)
            elif group == 1:
                proj(O_B, O_QKV)
            else:
                proj(O_G, O_AB)
                z_ab = _dot(st["xn"], win_ref[:, O_AB:O_AB + 2 * GDN_HEADS])
                z_scr[:, O_AB:] = jnp.pad(z_ab, ((0, 0), (0, LANES - 2 * GDN_HEADS)))

    def front_half():
        project(x_ref, (0, 1))
        yield
        conv_q(Q_COL)
        yield
        project(x_ref, (2,))
        yield
        conv_q(K_COL)
        yield
        conv_q(V_COL)
        yield
        conv_a()
        yield
        gates()
        yield
        for lo in range(0, len(problems), PREP_GROUP):
            for _ in _delta_prep(cfg, problems[lo:lo + PREP_GROUP], qkv_scr, gb_scr, gc_scr, st):
                yield
        per = cfg.chunk if cfg.carry else 2 * (VC_ROWS // GDN_HEADS)
        done = len(problems) * per
        if done < rows:
            o_scr[done:, :] = jnp.zeros((rows - done, GDN_HEADS * GDN_DV), F32)
            gated_norm(done, rows)
        yield
        for i in range(len(problems)):
            if cfg.carry:
                _delta_apply_carry(cfg, problems[i:i + 1], st["preps"][i:i + 1], s_scr, o_scr)
            else:
                yield from _delta_apply_blocks(cfg, problems[i:i + 1], st["preps"][i:i + 1], ssm_ref, nssm_ref, o_scr)
            gated_norm(i * per, (i + 1) * per)
            yield

    def back_half():
        h1 = padded_rows(xb_ref) + _dot(mix_scr[...], wout_ref[...])
        hn = _bf(_rms(h1, nffn_ref[...]))
        yield
        for c in range(FFN_HIDDEN // FFN_CHUNK):
            up = jnp.maximum(_dot(hn, wup_ref[:, c * FFN_CHUNK:(c + 1) * FFN_CHUNK]), 0.0)
            up_scr[:, c * FFN_CHUNK:(c + 1) * FFN_CHUNK] = _bf(up * up)
            yield
        outs = []
        for n in range(D_MODEL // FFN_CHUNK):
            cols = slice(n * FFN_CHUNK, (n + 1) * FFN_CHUNK)
            outs.append(h1[:, cols] + _dot(up_scr[...], wdown_ref[:, cols]))
            yield
        h2 = jnp.concatenate(outs, axis=1)
        y_ref[...] = _rms(h2, nfin_ref[...]).reshape(nb, sl, D_MODEL)[:, :cfg.x_len, :]

    back = back_half()
    if cfg.states_only:
        back = iter(())
        y_ref[...] = jnp.zeros(y_ref.shape, F32)
    if cfg.pipelined:
        next(back)
        for i, _ in enumerate(front_half()):
            for _ in range(cfg.back_after[i] if i < len(cfg.back_after) else 0):
                next(back, None)
    else:
        for _ in front_half():
            pass
    for _ in back:
        pass

    @pl.when(last)
    def _():
        end = cfg.valid_len - tail_lo
        nta_ref[...] = xa_buf[:, end - (SCONV_K - 1):end, :]
        ntq_ref[...] = xq_buf[:, end - (GDN_CONV_K - 1):end, :]
        if cfg.carry:
            nssm_ref[0] = s_scr[...]


def _layer_call(cfg, name, x, tail_a, tail_q, ssm, params):
    nb, sl, rows = cfg.nb, cfg.seq_len, cfg.rows
    assert x.shape == (cfg.batch, cfg.x_len * cfg.n_tt, D_MODEL), (x.shape, cfg)
    assert cfg.batch % nb == 0 and sl % SUBLANES == 0 and rows % VC_ROWS == 0
    assert not cfg.shared_state or nb == 1
    n_tiles, n_tt = cfg.n_tiles, cfg.n_tt

    def front_tile(s):
        return jnp.minimum(s, n_tiles - 1) if cfg.pipelined else s

    def back_tile(s):
        return jnp.maximum(s - 1, 0) if cfg.pipelined else s

    def tile_spec(which):
        return pl.BlockSpec((nb, cfg.x_len, D_MODEL), lambda s: (which(s) // n_tt, which(s) % n_tt, 0))

    def state_spec(shape):
        zeros = (0,) * len(shape)
        if cfg.shared_state:
            return pl.BlockSpec((nb,) + shape, lambda s: (0,) + zeros)
        return pl.BlockSpec((nb,) + shape, lambda s: (front_tile(s) // n_tt,) + zeros)

    def out_state_spec(shape):
        zeros = (0,) * len(shape)
        return pl.BlockSpec((nb,) + shape, lambda s: (front_tile(s) // n_tt,) + zeros)

    def const_spec(arr):
        zeros = (0,) * arr.ndim
        return pl.BlockSpec(arr.shape, lambda s: zeros, pipeline_mode=pl.Buffered(1))

    ta_shape, tq_shape = (SCONV_K - 1, CONV_CH), (GDN_CONV_K - 1, QKV_W)
    ssm_shape = (GDN_HEADS, GDN_DK, GDN_DV)
    in_specs = [tile_spec(front_tile), tile_spec(back_tile),
                state_spec(ta_shape), state_spec(tq_shape), state_spec(ssm_shape)]
    in_specs += [pl.BlockSpec(memory_space=pl.ANY) if cfg.states_only and i >= FIRST_BACK_PARAM else const_spec(p)
                 for i, p in enumerate(params)]
    out_specs = [tile_spec(back_tile), out_state_spec(ta_shape), out_state_spec(tq_shape), out_state_spec(ssm_shape)]
    out_shape = [
        jax.ShapeDtypeStruct(x.shape, F32),
        jax.ShapeDtypeStruct((cfg.batch,) + ta_shape, F32),
        jax.ShapeDtypeStruct((cfg.batch,) + tq_shape, F32),
        jax.ShapeDtypeStruct((cfg.batch,) + ssm_shape, F32),
    ]
    scratch = [
        pltpu.VMEM((rows, IN_W_PAD), F32),
        pltpu.VMEM((nb, SUBLANES, CONV_CH), F32),
        pltpu.VMEM((nb, SUBLANES, QKV_W), F32),
        pltpu.VMEM((rows, QKV_W), F32),
        pltpu.VMEM((rows, LANES), F32),
        pltpu.VMEM((rows, LANES), F32),
        pltpu.VMEM((rows, GDN_HEADS * GDN_DV), F32),
        pltpu.VMEM(ssm_shape, F32),
        pltpu.VMEM((rows, D_MODEL), BF16),
        pltpu.VMEM((rows, FFN_HIDDEN), BF16),
    ]
    return pl.pallas_call(
        functools.partial(_layer_kernel, cfg),
        grid=(n_tiles + (1 if cfg.pipelined else 0),),
        in_specs=in_specs,
        out_specs=out_specs,
        out_shape=out_shape,
        scratch_shapes=scratch,
        compiler_params=pltpu.CompilerParams(
            dimension_semantics=("arbitrary",), vmem_limit_bytes=VMEM_LIMIT_BYTES),
        name=name,
    )(x, x, tail_a, tail_q, ssm, *params)


META_PAD_LEN = 128
FIRST_BACK_PARAM = 7
PROMPT_TILE = 256
SAMPLE_PAD_LEN = 8
SAMPLE_SEQS_PER_STEP = 16


def kernel(x_prompt, x_sample, state_conv_a, state_conv_qkv, state_ssm, meta_tokens, norm_mix, w_in, conv_a_w,
           conv_qkv_w, a_log, dt_bias, gdn_norm, w_out, norm_ffn, w_up, w_down, norm_final):
    assert norm_mix.shape[0] == 1, "single layer"
    bp, tp, _ = x_prompt.shape
    bs, ts, _ = x_sample.shape
    n_meta = meta_tokens.shape[0]

    def row(v):
        return v.reshape(1, -1).astype(F32)

    def lane_row(v):
        return jnp.pad(v.astype(F32), (0, LANES - v.shape[0])).reshape(1, LANES)

    params = (
        row(norm_mix[0]),
        _bf(w_in[0]),
        conv_a_w[0].astype(F32), conv_qkv_w[0].astype(F32),
        lane_row(a_log[0]), lane_row(dt_bias[0]), row(gdn_norm[0]),
        _bf(w_out[0]), row(norm_ffn[0]),
        _bf(w_up[0]),
        _bf(w_down[0]),
        row(norm_final),
    )

    meta_cfg = Cfg(batch=1, nb=1, seq_len=META_PAD_LEN, valid_len=n_meta, chunk=64, n_tt=1, shared_state=False,
                   pipelined=False, states_only=True)
    meta_x = meta_tokens.astype(F32)[None]
    _, ta_m, tq_m, ssm_m = _layer_call(
        meta_cfg, "layer_meta", meta_x,
        jnp.zeros((1, SCONV_K - 1, CONV_CH), F32), jnp.zeros((1, GDN_CONV_K - 1, QKV_W), F32),
        jnp.zeros((1, GDN_HEADS, GDN_DK, GDN_DV), F32), params)

    prompt_cfg = Cfg(batch=bp, nb=1, seq_len=PROMPT_TILE, valid_len=PROMPT_TILE, chunk=64,
                     n_tt=tp // PROMPT_TILE, shared_state=True, pipelined=True, states_only=False,
                     back_after=PROMPT_BACK_AFTER)
    y_p, ta_p, tq_p, ssm_p = _layer_call(prompt_cfg, "layer_prompt", x_prompt, ta_m, tq_m, ssm_m, params)

    sample_cfg = Cfg(batch=bs, nb=SAMPLE_SEQS_PER_STEP, seq_len=SAMPLE_PAD_LEN, valid_len=ts, chunk=SAMPLE_PAD_LEN,
                     n_tt=1, shared_state=False, pipelined=False, states_only=False)
    y_s, ta_s, tq_s, ssm_s = _layer_call(
        sample_cfg, "layer_sample", x_sample, state_conv_a[0], state_conv_qkv[0], state_ssm[0], params)

    return (y_p, y_s, ta_p[None], tq_p[None], ssm_p[None], ta_s[None], tq_s[None], ssm_s[None])
```

```python
import dataclasses
import functools

import jax
import jax.numpy as jnp
from jax import lax
from jax.experimental import pallas as pl
from jax.experimental.pallas import tpu as pltpu

F32 = jnp.float32
BF16 = jnp.bfloat16

D_MODEL = 1024
CONV_CH = 512
SCONV_K = 3
GDN_HEADS = 4
GDN_DK = 128
GDN_DV = 128
GDN_CONV_K = 4
QKV_W = GDN_HEADS * (2 * GDN_DK + GDN_DV)
FFN_HIDDEN = 4 * D_MODEL
EPS = 1e-6

O_B = 0
O_C = O_B + CONV_CH
O_H = O_C + CONV_CH
O_QKV = O_H + CONV_CH
O_G = O_QKV + QKV_W
O_AB = O_G + GDN_HEADS * GDN_DV
LANES = 128
SUBLANES = 8
IN_W_PAD = O_AB + LANES
VC_ROWS = 128
NEUMANN_BASE = 16
FFN_CHUNK = 256
PREP_GROUP = 4
PROMPT_BACK_AFTER = (0, 0, 0, 3, 2, 3, 2, 2, 1, 1, 1, 1, 1, 0, 1, 1, 1, 1)
VMEM_LIMIT_BYTES = 56 * 1024 * 1024

Q_COL = 0
K_COL = GDN_HEADS * GDN_DK
V_COL = 2 * GDN_HEADS * GDN_DK


@dataclasses.dataclass(frozen=True)
class Cfg:
    batch: int
    nb: int
    seq_len: int
    valid_len: int
    chunk: int
    n_tt: int
    shared_state: bool
    pipelined: bool
    states_only: bool
    back_after: tuple = ()
    back_group: int = 1

    @property
    def rows(self):
        return self.nb * self.seq_len

    @property
    def n_tiles(self):
        return (self.batch // self.nb) * self.n_tt

    @property
    def x_len(self):
        return self.valid_len if self.n_tt == 1 else self.seq_len

    @property
    def carry(self):
        return self.nb == 1


def _bf(x):
    return x.astype(BF16)


def _dot(a, b):
    return jnp.dot(a, b, preferred_element_type=F32)


def _dot_nt(a, b):
    return lax.dot_general(a, b, (((1,), (1,)), ((), ())), preferred_element_type=F32)


def _rms(x, g_row):
    return x * lax.rsqrt(jnp.mean(x * x, axis=-1, keepdims=True) + EPS) * g_row


def _softplus(x):
    return jnp.maximum(x, 0.0) + jnp.log(1.0 + jnp.exp(-jnp.abs(x)))


def _silu(x):
    return x * jax.nn.sigmoid(x)


def _log2(n):
    k = n.bit_length() - 1
    assert (1 << k) == n, n
    return k


def _halves(x):
    return x[:, :LANES], x[:, LANES:]


def _pair(a, b):
    return jnp.concatenate([a, b], axis=1)


def _block_diag(y):
    ya, yb = _halves(y)
    z = jnp.zeros_like(ya)
    return jnp.concatenate([_pair(ya, z), _pair(z, yb)], axis=0)


def _pmm(x, y):
    return _dot(_bf(x), _block_diag(_bf(y)))


def _each(f, *lists):
    return [f(*xs) for xs in zip(*lists)]


def _problems(cfg):
    c = cfg.chunk
    if cfg.carry:
        assert 2 * c == VC_ROWS and GDN_HEADS == 4
        n = -(-cfg.valid_len // c)
        return [([(u * c, c, 0), (u * c, c, 2)], [(u * c, c, 1), (u * c, c, 3)]) for u in range(n)]
    half = VC_ROWS // GDN_HEADS
    assert half % cfg.seq_len == 0 and cfg.seq_len == c and cfg.rows % (2 * half) == 0
    return [([(u * 2 * half, half, h) for h in range(GDN_HEADS)],
             [(u * 2 * half + half, half, h) for h in range(GDN_HEADS)]) for u in range(cfg.rows // (2 * half))]


@dataclasses.dataclass
class Prep:
    u: jax.Array
    w: jax.Array
    qk: jax.Array
    q_dec: jax.Array
    k_dec: jax.Array
    g_last: tuple


def _delta_prep(cfg, problems, qkv_scr, gb_scr, gc_scr, out):
    c = cfg.chunk
    lc = _log2(c)
    ii = lax.broadcasted_iota(jnp.int32, (VC_ROWS, 2 * LANES), 0)
    jj = lax.broadcasted_iota(jnp.int32, (VC_ROWS, 2 * LANES), 1) & (LANES - 1)
    same = (ii >> lc) == (jj >> lc)
    lower = same & (jj <= ii)
    strict = same & (jj < ii)
    end_col = (jj == (ii | (c - 1)))[:, :LANES]
    eye = jnp.where(ii == jj, 1.0, 0.0)

    def stacked(ref, col, width, half):
        return jnp.concatenate([ref[r0:r0 + n, col(h):col(h) + width] for (r0, n, h) in half], axis=0)

    def packed(ref, base):
        return [_pair(*[stacked(ref, lambda h: base + LANES * h, LANES, half) for half in pr]) for pr in problems]

    def columns(ref, base):
        return [[stacked(ref, lambda h: base + h, 1, half) for half in pr] for pr in problems]

    def spread(cols):
        return _pair(*[jnp.broadcast_to(col, (VC_ROWS, LANES)) for col in cols])

    q, k, v = packed(qkv_scr, Q_COL), packed(qkv_scr, K_COL), packed(qkv_scr, V_COL)
    beta_c, gcum_c = columns(gb_scr, GDN_HEADS), columns(gc_scr, 0)
    beta, gcum = _each(spread, beta_c), _each(spread, gcum_c)
    g_rows = _each(lambda g: _pair(*[h.T for h in _halves(g)]), gcum)
    decay = _each(lambda g, gr: jnp.where(lower, jnp.exp(g - gr), 0.0), gcum, g_rows)
    g_last = _each(lambda gr: tuple(jnp.sum(jnp.where(end_col, h, 0.0), axis=1, keepdims=True) for h in _halves(gr)),
                   g_rows)
    kq = _each(lambda kk, qq: _dot_nt(_bf(jnp.concatenate([kk, qq], axis=0)), _block_diag(_bf(kk))), k, q)
    qk = _each(lambda m, d: m[VC_ROWS:] * d, kq, decay)
    a = _each(lambda m, b, d: jnp.where(strict, m[:VC_ROWS] * b * d, 0.0), kq, beta, decay)
    yield

    base = min(c, NEUMANN_BASE)
    lb = _log2(base)
    p = _each(lambda x: jnp.where((ii >> lb) == (jj >> lb), x, 0.0), a)
    t = _each(lambda x: eye - x, p)
    for _ in range(lb - 1):
        p = _each(_pmm, p, p)
        t = _each(lambda x, y: x + _pmm(x, y), t, p)
        yield
    size = base
    while size < c:
        ls = _log2(size)
        couple = ((ii >> (ls + 1)) == (jj >> (ls + 1))) & ((ii >> ls) != (jj >> ls))
        off = _each(lambda x: jnp.where(couple, x, 0.0), a)
        t_off = _each(_pmm, t, off)
        t = _each(lambda x, y: x - _pmm(y, x), t, t_off)
        size *= 2
        yield

    exp_g = _each(lambda cols: spread([jnp.exp(col) for col in cols]), gcum_c)
    k_scale = _each(lambda cols, gl: spread([jnp.exp(e - col) for col, e in zip(cols, gl)]), gcum_c, g_last)
    u = _each(lambda x, vv, b: _pmm(x, vv * b), t, v, beta)
    w = _each(lambda x, kk, b, e: _pmm(x, kk * (b * e)), t, k, beta, exp_g)
    q_dec = _each(lambda x, e: x * e, q, exp_g)
    k_dec = _each(lambda x, s: x * s, k, k_scale)
    out.setdefault("preps", []).extend(Prep(*xs) for xs in zip(u, w, qk, q_dec, k_dec, g_last))


def _delta_apply_carry(cfg, problems, preps, s_scr, o_scr):
    c = cfg.chunk
    top_rows = lax.broadcasted_iota(jnp.int32, (VC_ROWS, GDN_DV), 0) < c
    for (half_a, half_b), pr in zip(problems, preps):
        state = [s_scr[h] for h in range(GDN_HEADS)]
        w_s, q_s = [], []
        for blk in range(VC_ROWS // c):
            lo = blk * c
            wq = _bf(jnp.concatenate([pr.w[lo:lo + c], pr.q_dec[lo:lo + c]], axis=0))
            r = _dot(wq, _block_diag(_bf(_pair(state[half_a[blk][2]], state[half_b[blk][2]]))))
            w_s.append(r[:c])
            q_s.append(r[c:])
        v_new = pr.u - jnp.concatenate(w_s, axis=0)
        o = jnp.concatenate(q_s, axis=0) + _pmm(pr.qk, v_new)
        for half, vn, kd, gl, o_h in zip((half_a, half_b), _halves(v_new), _halves(pr.k_dec), pr.g_last, _halves(o)):
            upd = _dot(_bf(kd.T), _bf(_pair(jnp.where(top_rows, vn, 0.0), jnp.where(top_rows, 0.0, vn))))
            for blk, (r0, n, h) in enumerate(half):
                lo = blk * c
                s_scr[h] = state[h] * jnp.exp(gl[lo:lo + 1, :]) + upd[:, GDN_DV * blk:GDN_DV * (blk + 1)]
                o_scr[r0:r0 + n, GDN_DV * h:GDN_DV * (h + 1)] = o_h[lo:lo + c]


def _delta_apply_blocks(cfg, problems, preps, ssm_ref, nssm_ref, o_scr):
    c = cfg.chunk
    lc = _log2(c)
    blk_id = lax.broadcasted_iota(jnp.int32, (VC_ROWS, GDN_DV), 0) >> lc
    for pieces, pr in zip(problems, preps):
        halves = []
        for half, w_h, qd_h, u_h in zip(pieces, _halves(pr.w), _halves(pr.q_dec), _halves(pr.u)):
            blocks = [(r0 + t0, h, (r0 + t0) // cfg.seq_len) for (r0, n, h) in half for t0 in range(0, n, c)]
            vs, qs = [], []
            for b, (_, h, seq) in enumerate(blocks):
                lo = b * c
                r = _dot(_bf(jnp.concatenate([w_h[lo:lo + c], qd_h[lo:lo + c]], axis=0)), _bf(ssm_ref[seq, h]))
                vs.append(u_h[lo:lo + c] - r[:c])
                qs.append(r[c:])
            halves.append((blocks, jnp.concatenate(vs, axis=0), jnp.concatenate(qs, axis=0)))
            yield
        v_new = _pair(halves[0][1], halves[1][1])
        o = _pair(halves[0][2], halves[1][2]) + _pmm(pr.qk, v_new)
        for (blocks, vn, _), kd, gl, o_h in zip(halves, _halves(pr.k_dec), pr.g_last, _halves(o)):
            kd_t = _bf(kd.T)
            for b0 in range(0, len(blocks), 2):
                upd = _dot(kd_t, _bf(_pair(jnp.where(blk_id == b0, vn, 0.0), jnp.where(blk_id == b0 + 1, vn, 0.0))))
                for b in (b0, b0 + 1):
                    row, h, seq = blocks[b]
                    lo = b * c
                    nssm_ref[seq, h] = (ssm_ref[seq, h] * jnp.exp(gl[lo:lo + 1, :])
                                        + upd[:, GDN_DV * (b - b0):GDN_DV * (b - b0 + 1)])
                    o_scr[row:row + c, GDN_DV * h:GDN_DV * (h + 1)] = o_h[lo:lo + c]
                if b0 % 8 == 6:
                    yield


def _layer_kernel(cfg, x_ref, xb_ref, ta_ref, tq_ref, ssm_ref, nmix_ref, win_ref, caw_ref, cqw_ref, alog_ref, dt_ref,
                  gnorm_ref, wout_ref, nffn_ref, wup_ref, wdown_ref, nfin_ref,
                  y_ref, nta_ref, ntq_ref, nssm_ref,
                  z_scr, xa_buf, xq_buf, qkv_scr, gb_scr, gc_scr, o_scr, s_scr, mix_scr, up_scr):
    step = pl.program_id(0)
    nb, sl, rows = cfg.nb, cfg.seq_len, cfg.rows
    if cfg.pipelined:
        tile = jnp.minimum(step, cfg.n_tiles - 1)
        real = step < cfg.n_tiles
    else:
        tile = step
        real = True
    j = lax.rem(tile, cfg.n_tt)
    first = j == 0
    last = (j == cfg.n_tt - 1) & real

    @pl.when(first)
    def _():
        xa_buf[...] = jnp.zeros(xa_buf.shape, F32)
        xq_buf[...] = jnp.zeros(xq_buf.shape, F32)
        xa_buf[:, SUBLANES - (SCONV_K - 1):, :] = ta_ref[...]
        xq_buf[:, SUBLANES - (GDN_CONV_K - 1):, :] = tq_ref[...]
        if cfg.carry:
            s_scr[...] = ssm_ref[0]

    st = {}
    problems = _problems(cfg)

    def padded_rows(ref):
        x = ref[...]
        if cfg.x_len < sl:
            x = jnp.concatenate([x, jnp.zeros((x.shape[0], sl - cfg.x_len, D_MODEL), F32)], axis=1)
        return x.reshape(x.shape[0] * sl, D_MODEL)

    grouped = cfg.back_group > 1
    if grouped:
        assert not cfg.pipelined and not cfg.states_only and sl == SUBLANES and 2 * cfg.valid_len == sl
        mix_base = pl.multiple_of(lax.rem(step, cfg.back_group) * rows, rows)

    def mix_store(lo, hi, col_lo, col_hi, value):
        if grouped:
            mix_scr[pl.ds(mix_base + lo, hi - lo), col_lo:col_hi] = value
        else:
            mix_scr[lo:hi, col_lo:col_hi] = _bf(value)

    def proj(lo, hi):
        z_scr[:, lo:hi] = _dot(st["xn"], win_ref[:, lo:hi])

    if cfg.pipelined:
        @pl.when(step == 0)
        def _():
            mix_scr[...] = jnp.zeros(mix_scr.shape, mix_scr.dtype)

    tail_lo = (cfg.valid_len - 1) // SUBLANES * SUBLANES
    assert cfg.valid_len - tail_lo >= GDN_CONV_K - 1 and (cfg.n_tt == 1 or cfg.valid_len == sl)

    def causal_conv(x, prev_buf, w_ref, col, width):
        taps = w_ref.shape[0]
        t8 = lax.broadcasted_iota(jnp.int32, (nb, SUBLANES, width), 1)
        prev = prev_buf[:, :, col:col + width]
        acc = x * w_ref[taps - 1:taps, col:col + width]
        for back in range(1, taps):
            moved = pltpu.roll(x, back, 1)
            head = jnp.where(t8 < back, pltpu.roll(prev, back, 1), moved[:, :SUBLANES, :])
            moved = head if sl == SUBLANES else jnp.concatenate([head, moved[:, SUBLANES:, :]], axis=1)
            acc = acc + moved * w_ref[taps - 1 - back:taps - back, col:col + width]
        prev_buf[:, :, col:col + width] = x[:, tail_lo:tail_lo + SUBLANES, :]
        return acc

    def conv_q(col):
        width = GDN_HEADS * GDN_DK
        x = z_scr[:, O_QKV + col:O_QKV + col + width].reshape(nb, sl, width)
        act = _silu(causal_conv(x, xq_buf, cqw_ref, col, width).reshape(rows, width))
        if col == V_COL:
            qkv_scr[:, col:col + width] = act
            return
        scale = GDN_DK ** -0.5 if col == Q_COL else 1.0
        for h in range(GDN_HEADS):
            ah = act[:, GDN_DK * h:GDN_DK * (h + 1)]
            qkv_scr[:, col + GDN_DK * h:col + GDN_DK * (h + 1)] = (
                ah * (lax.rsqrt(jnp.sum(ah * ah, axis=-1, keepdims=True) + EPS) * scale))

    def conv_a():
        x = (z_scr[:, O_C:O_H] * z_scr[:, O_H:O_QKV]).reshape(nb, sl, CONV_CH)
        acc = causal_conv(x, xa_buf, caw_ref, 0, CONV_CH)
        mix_store(0, rows, 0, CONV_CH, z_scr[:, O_B:O_C] * acc.reshape(rows, CONV_CH))

    def gates():
        zab = z_scr[:, O_AB:IN_W_PAD]
        g = -jnp.exp(alog_ref[...]) * _softplus(zab + dt_ref[...])
        beta = jax.nn.sigmoid(zab)
        lane = lax.broadcasted_iota(jnp.int32, (rows, LANES), 1)
        gb = jnp.where(lane < GDN_HEADS, g, beta)
        if cfg.valid_len < sl:
            t_in_seq = lax.broadcasted_iota(jnp.int32, (rows, LANES), 0) & (sl - 1)
            gb = jnp.where(t_in_seq < cfg.valid_len, gb, 0.0)
        gb_scr[...] = gb
        ri = lax.broadcasted_iota(jnp.int32, (rows, rows), 0)
        rj = lax.broadcasted_iota(jnp.int32, (rows, rows), 1)
        lc = _log2(cfg.chunk)
        tri = jnp.where(((ri >> lc) == (rj >> lc)) & (rj <= ri), 1.0, 0.0).astype(BF16)
        g1 = _bf(gb)
        r1 = gb - g1.astype(F32)
        g2 = _bf(r1)
        g3 = _bf(r1 - g2.astype(F32))
        gc_scr[...] = _dot(tri, g1) + (_dot(tri, g2) + _dot(tri, g3))

    def gated_norm(lo, hi):
        for h in range(GDN_HEADS):
            oh = o_scr[lo:hi, GDN_DV * h:GDN_DV * (h + 1)]
            gate = z_scr[lo:hi, O_G + GDN_DV * h:O_G + GDN_DV * (h + 1)]
            mix_store(lo, hi, CONV_CH + GDN_DV * h, CONV_CH + GDN_DV * (h + 1),
                      _rms(oh, gnorm_ref[...]) * _silu(gate))

    def project(src_ref, groups):
        if "xn" not in st:
            st["xn"] = _bf(_rms(padded_rows(src_ref), nmix_ref[...]))
        for group in groups:
            if group == 0:
                proj(O_QKV, O_G)
            elif group == 1:
                proj(O_B, O_QKV)
            else:
                proj(O_G, O_AB)
                z_ab = _dot(st["xn"], win_ref[:, O_AB:O_AB + 2 * GDN_HEADS])
                z_scr[:, O_AB:] = jnp.pad(z_ab, ((0, 0), (0, LANES - 2 * GDN_HEADS)))

    def front_half():
        project(x_ref, (0, 1))
        yield
        conv_q(Q_COL)
        yield
        project(x_ref, (2,))
        yield
        conv_q(K_COL)
        yield
        conv_q(V_COL)
        yield
        conv_a()
        yield
        gates()
        yield
        for lo in range(0, len(problems), PREP_GROUP):
            for _ in _delta_prep(cfg, problems[lo:lo + PREP_GROUP], qkv_scr, gb_scr, gc_scr, st):
                yield
        per = cfg.chunk if cfg.carry else 2 * (VC_ROWS // GDN_HEADS)
        done = len(problems) * per
        if done < rows:
            o_scr[done:, :] = jnp.zeros((rows - done, GDN_HEADS * GDN_DV), F32)
            gated_norm(done, rows)
        yield
        for i in range(len(problems)):
            if cfg.carry:
                _delta_apply_carry(cfg, problems[i:i + 1], st["preps"][i:i + 1], s_scr, o_scr)
            else:
                yield from _delta_apply_blocks(cfg, problems[i:i + 1], st["preps"][i:i + 1], ssm_ref, nssm_ref, o_scr)
            gated_norm(i * per, (i + 1) * per)
            yield

    def real_rows(v):
        g = v.reshape(v.shape[0] // (2 * sl), 2 * sl, v.shape[1])
        t8 = lax.broadcasted_iota(jnp.int32, (g.shape[0], sl, g.shape[2]), 1)
        return jnp.where(t8 < cfg.valid_len, g[:, :sl], pltpu.roll(g[:, sl:], cfg.valid_len, 1)).reshape(
            v.shape[0] // 2, v.shape[1])

    def padded_seqs(v):
        g = v.reshape(v.shape[0] // sl, sl, v.shape[1])
        return jnp.concatenate([g, pltpu.roll(g, cfg.valid_len, 1)], axis=1).reshape(
            v.shape[0] // cfg.valid_len, sl, v.shape[1])

    def back_half():
        if grouped:
            h1 = real_rows(padded_rows(xb_ref)) + _dot(_bf(real_rows(mix_scr[...])), wout_ref[...])
        else:
            h1 = padded_rows(xb_ref) + _dot(mix_scr[...], wout_ref[...])
        hn = _bf(_rms(h1, nffn_ref[...]))
        yield
        for c in range(FFN_HIDDEN // FFN_CHUNK):
            up = jnp.maximum(_dot(hn, wup_ref[:, c * FFN_CHUNK:(c + 1) * FFN_CHUNK]), 0.0)
            up_scr[:, c * FFN_CHUNK:(c + 1) * FFN_CHUNK] = _bf(up * up)
            yield
        outs = []
        for n in range(D_MODEL // FFN_CHUNK):
            cols = slice(n * FFN_CHUNK, (n + 1) * FFN_CHUNK)
            outs.append(h1[:, cols] + _dot(up_scr[...], wdown_ref[:, cols]))
            yield
        h2 = jnp.concatenate(outs, axis=1)
        y = _rms(h2, nfin_ref[...])
        y = padded_seqs(y) if grouped else y.reshape(nb, sl, D_MODEL)
        y_ref[...] = y[:, :cfg.x_len, :]

    back = back_half()
    if cfg.states_only:
        back = iter(())
        y_ref[...] = jnp.zeros(y_ref.shape, F32)
    if cfg.pipelined:
        next(back)
        for i, _ in enumerate(front_half()):
            for _ in range(cfg.back_after[i] if i < len(cfg.back_after) else 0):
                next(back, None)
    else:
        for _ in front_half():
            pass
    if grouped:
        @pl.when(lax.rem(step, cfg.back_group) == cfg.back_group - 1)
        def _():
            for _ in back:
                pass
    else:
        for _ in back:
            pass

    @pl.when(last)
    def _():
        end = cfg.valid_len - tail_lo
        nta_ref[...] = xa_buf[:, end - (SCONV_K - 1):end, :]
        ntq_ref[...] = xq_buf[:, end - (GDN_CONV_K - 1):end, :]
        if cfg.carry:
            nssm_ref[0] = s_scr[...]


def _layer_call(cfg, name, x, tail_a, tail_q, ssm, params):
    nb, sl, rows = cfg.nb, cfg.seq_len, cfg.rows
    assert x.shape == (cfg.batch, cfg.x_len * cfg.n_tt, D_MODEL), (x.shape, cfg)
    assert cfg.batch % nb == 0 and sl % SUBLANES == 0 and rows % VC_ROWS == 0
    assert not cfg.shared_state or nb == 1
    n_tiles, n_tt = cfg.n_tiles, cfg.n_tt

    def front_tile(s):
        return jnp.minimum(s, n_tiles - 1) if cfg.pipelined else s

    def back_tile(s):
        return jnp.maximum(s - 1, 0) if cfg.pipelined else s

    def tile_spec(which):
        return pl.BlockSpec((nb, cfg.x_len, D_MODEL), lambda s: (which(s) // n_tt, which(s) % n_tt, 0))

    group = cfg.back_group
    assert group == 1 or (n_tt == 1 and n_tiles % group == 0)
    back_spec = tile_spec(back_tile) if group == 1 else pl.BlockSpec(
        (nb * group, cfg.x_len, D_MODEL), lambda s: (s // group, 0, 0))

    def state_spec(shape):
        zeros = (0,) * len(shape)
        if cfg.shared_state:
            return pl.BlockSpec((nb,) + shape, lambda s: (0,) + zeros)
        return pl.BlockSpec((nb,) + shape, lambda s: (front_tile(s) // n_tt,) + zeros)

    def out_state_spec(shape):
        zeros = (0,) * len(shape)
        return pl.BlockSpec((nb,) + shape, lambda s: (front_tile(s) // n_tt,) + zeros)

    def const_spec(arr):
        zeros = (0,) * arr.ndim
        return pl.BlockSpec(arr.shape, lambda s: zeros, pipeline_mode=pl.Buffered(1))

    ta_shape, tq_shape = (SCONV_K - 1, CONV_CH), (GDN_CONV_K - 1, QKV_W)
    ssm_shape = (GDN_HEADS, GDN_DK, GDN_DV)
    in_specs = [tile_spec(front_tile), back_spec,
                state_spec(ta_shape), state_spec(tq_shape), state_spec(ssm_shape)]
    in_specs += [pl.BlockSpec(memory_space=pl.ANY) if cfg.states_only and i >= FIRST_BACK_PARAM else const_spec(p)
                 for i, p in enumerate(params)]
    out_specs = [back_spec, out_state_spec(ta_shape), out_state_spec(tq_shape), out_state_spec(ssm_shape)]
    out_shape = [
        jax.ShapeDtypeStruct(x.shape, F32),
        jax.ShapeDtypeStruct((cfg.batch,) + ta_shape, F32),
        jax.ShapeDtypeStruct((cfg.batch,) + tq_shape, F32),
        jax.ShapeDtypeStruct((cfg.batch,) + ssm_shape, F32),
    ]
    scratch = [
        pltpu.VMEM((rows, IN_W_PAD), F32),
        pltpu.VMEM((nb, SUBLANES, CONV_CH), F32),
        pltpu.VMEM((nb, SUBLANES, QKV_W), F32),
        pltpu.VMEM((rows, QKV_W), F32),
        pltpu.VMEM((rows, LANES), F32),
        pltpu.VMEM((rows, LANES), F32),
        pltpu.VMEM((rows, GDN_HEADS * GDN_DV), F32),
        pltpu.VMEM(ssm_shape, F32),
        pltpu.VMEM((rows * group, D_MODEL), BF16 if group == 1 else F32),
        pltpu.VMEM((rows if group == 1 else rows * group * cfg.valid_len // sl, FFN_HIDDEN), BF16),
    ]
    return pl.pallas_call(
        functools.partial(_layer_kernel, cfg),
        grid=(n_tiles + (1 if cfg.pipelined else 0),),
        in_specs=in_specs,
        out_specs=out_specs,
        out_shape=out_shape,
        scratch_shapes=scratch,
        compiler_params=pltpu.CompilerParams(
            dimension_semantics=("arbitrary",), vmem_limit_bytes=VMEM_LIMIT_BYTES),
        name=name,
    )(x, x, tail_a, tail_q, ssm, *params)


META_PAD_LEN = 128
FIRST_BACK_PARAM = 7
PROMPT_TILE = 256
SAMPLE_PAD_LEN = 8
SAMPLE_SEQS_PER_STEP = 16


def kernel(x_prompt, x_sample, state_conv_a, state_conv_qkv, state_ssm, meta_tokens, norm_mix, w_in, conv_a_w,
           conv_qkv_w, a_log, dt_bias, gdn_norm, w_out, norm_ffn, w_up, w_down, norm_final):
    assert norm_mix.shape[0] == 1, "single layer"
    bp, tp, _ = x_prompt.shape
    bs, ts, _ = x_sample.shape
    n_meta = meta_tokens.shape[0]

    def row(v):
        return v.reshape(1, -1).astype(F32)

    def lane_row(v):
        return jnp.pad(v.astype(F32), (0, LANES - v.shape[0])).reshape(1, LANES)

    params = (
        row(norm_mix[0]),
        _bf(w_in[0]),
        conv_a_w[0].astype(F32), conv_qkv_w[0].astype(F32),
        lane_row(a_log[0]), lane_row(dt_bias[0]), row(gdn_norm[0]),
        _bf(w_out[0]), row(norm_ffn[0]),
        _bf(w_up[0]),
        _bf(w_down[0]),
        row(norm_final),
    )

    meta_cfg = Cfg(batch=1, nb=1, seq_len=META_PAD_LEN, valid_len=n_meta, chunk=64, n_tt=1, shared_state=False,
                   pipelined=False, states_only=True)
    meta_x = meta_tokens.astype(F32)[None]
    _, ta_m, tq_m, ssm_m = _layer_call(
        meta_cfg, "layer_meta", meta_x,
        jnp.zeros((1, SCONV_K - 1, CONV_CH), F32), jnp.zeros((1, GDN_CONV_K - 1, QKV_W), F32),
        jnp.zeros((1, GDN_HEADS, GDN_DK, GDN_DV), F32), params)

    prompt_cfg = Cfg(batch=bp, nb=1, seq_len=PROMPT_TILE, valid_len=PROMPT_TILE, chunk=64,
                     n_tt=tp // PROMPT_TILE, shared_state=True, pipelined=True, states_only=False,
                     back_after=PROMPT_BACK_AFTER)
    y_p, ta_p, tq_p, ssm_p = _layer_call(prompt_cfg, "layer_prompt", x_prompt, ta_m, tq_m, ssm_m, params)

    sample_cfg = Cfg(batch=bs, nb=SAMPLE_SEQS_PER_STEP, seq_len=SAMPLE_PAD_LEN, valid_len=ts, chunk=SAMPLE_PAD_LEN,
                     n_tt=1, shared_state=False, pipelined=False, states_only=False, back_group=2)
    y_s, ta_s, tq_s, ssm_s = _layer_call(
        sample_cfg, "layer_sample", x_sample, state_conv_a[0], state_conv_qkv[0], state_ssm[0], params)

    return (y_p, y_s, ta_p[None], tq_p[None], ssm_p[None], ta_s[None], tq_s[None], ssm_s[None])
```

```python
import dataclasses
import functools

import jax
import jax.numpy as jnp
from jax import lax
from jax.experimental import pallas as pl
from jax.experimental.pallas import tpu as pltpu

F32 = jnp.float32
BF16 = jnp.bfloat16

D_MODEL = 1024
CONV_CH = 512
SCONV_K = 3
GDN_HEADS = 4
GDN_DK = 128
GDN_DV = 128
GDN_CONV_K = 4
QKV_W = GDN_HEADS * (2 * GDN_DK + GDN_DV)
FFN_HIDDEN = 4 * D_MODEL
EPS = 1e-6

O_B = 0
O_C = O_B + CONV_CH
O_H = O_C + CONV_CH
O_QKV = O_H + CONV_CH
O_G = O_QKV + QKV_W
O_AB = O_G + GDN_HEADS * GDN_DV
LANES = 128
SUBLANES = 8
IN_W_PAD = O_AB + LANES
VC_ROWS = 128
NEUMANN_BASE = 8
FFN_CHUNK = 256
PREP_GROUP = 4
PROMPT_BACK_AFTER = (0, 0, 0, 3, 2, 3, 2, 2, 1, 1, 1, 1, 1, 0, 1, 1, 1, 1)
VMEM_LIMIT_BYTES = 56 * 1024 * 1024

Q_COL = 0
K_COL = GDN_HEADS * GDN_DK
V_COL = 2 * GDN_HEADS * GDN_DK


@dataclasses.dataclass(frozen=True)
class Cfg:
    batch: int
    nb: int
    seq_len: int
    valid_len: int
    chunk: int
    n_tt: int
    shared_state: bool
    pipelined: bool
    states_only: bool
    back_after: tuple = ()
    back_group: int = 1

    @property
    def rows(self):
        return self.nb * self.seq_len

    @property
    def n_tiles(self):
        return (self.batch // self.nb) * self.n_tt

    @property
    def x_len(self):
        return self.valid_len if self.n_tt == 1 else self.seq_len

    @property
    def carry(self):
        return self.nb == 1


def _bf(x):
    return x.astype(BF16)


def _dot(a, b):
    return jnp.dot(a, b, preferred_element_type=F32)


def _dot_nt(a, b):
    return lax.dot_general(a, b, (((1,), (1,)), ((), ())), preferred_element_type=F32)


def _rms(x, g_row):
    return x * lax.rsqrt(jnp.mean(x * x, axis=-1, keepdims=True) + EPS) * g_row


def _softplus(x):
    return jnp.maximum(x, 0.0) + jnp.log(1.0 + jnp.exp(-jnp.abs(x)))


def _silu(x):
    return x * jax.nn.sigmoid(x)


def _log2(n):
    k = n.bit_length() - 1
    assert (1 << k) == n, n
    return k


def _halves(x):
    return x[:, :LANES], x[:, LANES:]


def _pair(a, b):
    return jnp.concatenate([a, b], axis=1)


def _block_diag(y):
    ya, yb = _halves(y)
    z = jnp.zeros_like(ya)
    return jnp.concatenate([_pair(ya, z), _pair(z, yb)], axis=0)


def _pmm(x, y):
    return _dot(_bf(x), _block_diag(_bf(y)))


def _each(f, *lists):
    return [f(*xs) for xs in zip(*lists)]


def _problems(cfg):
    c = cfg.chunk
    if cfg.carry:
        assert 2 * c == VC_ROWS and GDN_HEADS == 4
        n = -(-cfg.valid_len // c)
        return [([(u * c, c, 0), (u * c, c, 2)], [(u * c, c, 1), (u * c, c, 3)]) for u in range(n)]
    half = VC_ROWS // GDN_HEADS
    assert half % cfg.seq_len == 0 and cfg.seq_len == c and cfg.rows % (2 * half) == 0
    return [([(u * 2 * half, half, h) for h in range(GDN_HEADS)],
             [(u * 2 * half + half, half, h) for h in range(GDN_HEADS)]) for u in range(cfg.rows // (2 * half))]


@dataclasses.dataclass
class Prep:
    u: jax.Array
    w: jax.Array
    qk: jax.Array
    q_dec: jax.Array
    k_dec: jax.Array
    g_last: tuple


def _delta_prep(cfg, problems, qkv_scr, gb_scr, gc_scr, out):
    c = cfg.chunk
    lc = _log2(c)
    ii = lax.broadcasted_iota(jnp.int32, (VC_ROWS, 2 * LANES), 0)
    jj = lax.broadcasted_iota(jnp.int32, (VC_ROWS, 2 * LANES), 1) & (LANES - 1)
    same = (ii >> lc) == (jj >> lc)
    lower = same & (jj <= ii)
    strict = same & (jj < ii)
    end_col = (jj == (ii | (c - 1)))[:, :LANES]
    eye = jnp.where(ii == jj, 1.0, 0.0)

    def stacked(ref, col, width, half):
        return jnp.concatenate([ref[r0:r0 + n, col(h):col(h) + width] for (r0, n, h) in half], axis=0)

    def packed(ref, base):
        return [_pair(*[stacked(ref, lambda h: base + LANES * h, LANES, half) for half in pr]) for pr in problems]

    def columns(ref, base):
        return [[stacked(ref, lambda h: base + h, 1, half) for half in pr] for pr in problems]

    def spread(cols):
        return _pair(*[jnp.broadcast_to(col, (VC_ROWS, LANES)) for col in cols])

    q, k, v = packed(qkv_scr, Q_COL), packed(qkv_scr, K_COL), packed(qkv_scr, V_COL)
    beta_c, gcum_c = columns(gb_scr, GDN_HEADS), columns(gc_scr, 0)
    beta, gcum = _each(spread, beta_c), _each(spread, gcum_c)
    g_rows = _each(lambda g: _pair(*[h.T for h in _halves(g)]), gcum)
    decay = _each(lambda g, gr: jnp.where(lower, jnp.exp(g - gr), 0.0), gcum, g_rows)
    g_last = _each(lambda gr: tuple(jnp.sum(jnp.where(end_col, h, 0.0), axis=1, keepdims=True) for h in _halves(gr)),
                   g_rows)
    kq = _each(lambda kk, qq: _dot_nt(_bf(jnp.concatenate([kk, qq], axis=0)), _block_diag(_bf(kk))), k, q)
    qk = _each(lambda m, d: m[VC_ROWS:] * d, kq, decay)
    a = _each(lambda m, b, d: jnp.where(strict, m[:VC_ROWS] * b * d, 0.0), kq, beta, decay)
    yield

    base = min(c, NEUMANN_BASE)
    lb = _log2(base)
    p = _each(lambda x: jnp.where((ii >> lb) == (jj >> lb), x, 0.0), a)
    t = _each(lambda x: eye - x, p)
    for _ in range(lb - 1):
        p = _each(_pmm, p, p)
        t = _each(lambda x, y: x + _pmm(x, y), t, p)
        yield
    size = base
    while size < c:
        ls = _log2(size)
        couple = ((ii >> (ls + 1)) == (jj >> (ls + 1))) & ((ii >> ls) != (jj >> ls))
        off = _each(lambda x: jnp.where(couple, x, 0.0), a)
        t_off = _each(_pmm, t, off)
        t = _each(lambda x, y: x - _pmm(y, x), t, t_off)
        size *= 2
        yield

    exp_g = _each(lambda cols: spread([jnp.exp(col) for col in cols]), gcum_c)
    k_scale = _each(lambda cols, gl: spread([jnp.exp(e - col) for col, e in zip(cols, gl)]), gcum_c, g_last)
    u = _each(lambda x, vv, b: _pmm(x, vv * b), t, v, beta)
    w = _each(lambda x, kk, b, e: _pmm(x, kk * (b * e)), t, k, beta, exp_g)
    q_dec = _each(lambda x, e: x * e, q, exp_g)
    k_dec = _each(lambda x, s: x * s, k, k_scale)
    out.setdefault("preps", []).extend(Prep(*xs) for xs in zip(u, w, qk, q_dec, k_dec, g_last))


def _delta_apply_carry(cfg, problems, preps, s_scr, o_scr):
    c = cfg.chunk
    top_rows = lax.broadcasted_iota(jnp.int32, (VC_ROWS, GDN_DV), 0) < c
    for (half_a, half_b), pr in zip(problems, preps):
        state = [s_scr[h] for h in range(GDN_HEADS)]
        w_s, q_s = [], []
        for blk in range(VC_ROWS // c):
            lo = blk * c
            wq = _bf(jnp.concatenate([pr.w[lo:lo + c], pr.q_dec[lo:lo + c]], axis=0))
            r = _dot(wq, _block_diag(_bf(_pair(state[half_a[blk][2]], state[half_b[blk][2]]))))
            w_s.append(r[:c])
            q_s.append(r[c:])
        v_new = pr.u - jnp.concatenate(w_s, axis=0)
        o = jnp.concatenate(q_s, axis=0) + _pmm(pr.qk, v_new)
        for half, vn, kd, gl, o_h in zip((half_a, half_b), _halves(v_new), _halves(pr.k_dec), pr.g_last, _halves(o)):
            upd = _dot(_bf(kd.T), _bf(_pair(jnp.where(top_rows, vn, 0.0), jnp.where(top_rows, 0.0, vn))))
            for blk, (r0, n, h) in enumerate(half):
                lo = blk * c
                s_scr[h] = state[h] * jnp.exp(gl[lo:lo + 1, :]) + upd[:, GDN_DV * blk:GDN_DV * (blk + 1)]
                o_scr[r0:r0 + n, GDN_DV * h:GDN_DV * (h + 1)] = o_h[lo:lo + c]


def _delta_apply_blocks(cfg, problems, preps, ssm_ref, nssm_ref, o_scr):
    c = cfg.chunk
    lc = _log2(c)
    blk_id = lax.broadcasted_iota(jnp.int32, (VC_ROWS, GDN_DV), 0) >> lc
    for pieces, pr in zip(problems, preps):
        halves = []
        for half, w_h, qd_h, u_h in zip(pieces, _halves(pr.w), _halves(pr.q_dec), _halves(pr.u)):
            blocks = [(r0 + t0, h, (r0 + t0) // cfg.seq_len) for (r0, n, h) in half for t0 in range(0, n, c)]
            vs, qs = [], []
            for b, (_, h, seq) in enumerate(blocks):
                lo = b * c
                r = _dot(_bf(jnp.concatenate([w_h[lo:lo + c], qd_h[lo:lo + c]], axis=0)), _bf(ssm_ref[seq, h]))
                vs.append(u_h[lo:lo + c] - r[:c])
                qs.append(r[c:])
            halves.append((blocks, jnp.concatenate(vs, axis=0), jnp.concatenate(qs, axis=0)))
            yield
        v_new = _pair(halves[0][1], halves[1][1])
        o = _pair(halves[0][2], halves[1][2]) + _pmm(pr.qk, v_new)
        for (blocks, vn, _), kd, gl, o_h in zip(halves, _halves(pr.k_dec), pr.g_last, _halves(o)):
            kd_t = _bf(kd.T)
            for b0 in range(0, len(blocks), 2):
                upd = _dot(kd_t, _bf(_pair(jnp.where(blk_id == b0, vn, 0.0), jnp.where(blk_id == b0 + 1, vn, 0.0))))
                for b in (b0, b0 + 1):
                    row, h, seq = blocks[b]
                    lo = b * c
                    nssm_ref[seq, h] = (ssm_ref[seq, h] * jnp.exp(gl[lo:lo + 1, :])
                                        + upd[:, GDN_DV * (b - b0):GDN_DV * (b - b0 + 1)])
                    o_scr[row:row + c, GDN_DV * h:GDN_DV * (h + 1)] = o_h[lo:lo + c]
                if b0 % 8 == 6:
                    yield


def _layer_kernel(cfg, x_ref, xb_ref, ta_ref, tq_ref, ssm_ref, nmix_ref, win_ref, caw_ref, cqw_ref, alog_ref, dt_ref,
                  gnorm_ref, wout_ref, nffn_ref, wup_ref, wdown_ref, nfin_ref,
                  y_ref, nta_ref, ntq_ref, nssm_ref,
                  z_scr, xa_buf, xq_buf, qkv_scr, gb_scr, gc_scr, o_scr, s_scr, mix_scr, up_scr):
    step = pl.program_id(0)
    nb, sl, rows = cfg.nb, cfg.seq_len, cfg.rows
    if cfg.pipelined:
        tile = jnp.minimum(step, cfg.n_tiles - 1)
        real = step < cfg.n_tiles
    else:
        tile = step
        real = True
    j = lax.rem(tile, cfg.n_tt)
    first = j == 0
    last = (j == cfg.n_tt - 1) & real

    @pl.when(first)
    def _():
        xa_buf[...] = jnp.zeros(xa_buf.shape, F32)
        xq_buf[...] = jnp.zeros(xq_buf.shape, F32)
        xa_buf[:, SUBLANES - (SCONV_K - 1):, :] = ta_ref[...]
        xq_buf[:, SUBLANES - (GDN_CONV_K - 1):, :] = tq_ref[...]
        if cfg.carry:
            s_scr[...] = ssm_ref[0]

    st = {}
    problems = _problems(cfg)

    def padded_rows(ref):
        x = ref[...]
        if cfg.x_len < sl:
            x = jnp.concatenate([x, jnp.zeros((x.shape[0], sl - cfg.x_len, D_MODEL), F32)], axis=1)
        return x.reshape(x.shape[0] * sl, D_MODEL)

    grouped = cfg.back_group > 1
    if grouped:
        assert not cfg.pipelined and not cfg.states_only and sl == SUBLANES and 2 * cfg.valid_len == sl
        mix_base = pl.multiple_of(lax.rem(step, cfg.back_group) * rows, rows)

    def mix_store(lo, hi, col_lo, col_hi, value):
        if grouped:
            mix_scr[pl.ds(mix_base + lo, hi - lo), col_lo:col_hi] = value
        else:
            mix_scr[lo:hi, col_lo:col_hi] = _bf(value)

    def proj(lo, hi):
        z_scr[:, lo:hi] = _dot(st["xn"], win_ref[:, lo:hi])

    if cfg.pipelined:
        @pl.when(step == 0)
        def _():
            mix_scr[...] = jnp.zeros(mix_scr.shape, mix_scr.dtype)

    tail_lo = (cfg.valid_len - 1) // SUBLANES * SUBLANES
    assert cfg.valid_len - tail_lo >= GDN_CONV_K - 1 and (cfg.n_tt == 1 or cfg.valid_len == sl)

    def causal_conv(x, prev_buf, w_ref, col, width):
        taps = w_ref.shape[0]
        t8 = lax.broadcasted_iota(jnp.int32, (nb, SUBLANES, width), 1)
        prev = prev_buf[:, :, col:col + width]
        acc = x * w_ref[taps - 1:taps, col:col + width]
        for back in range(1, taps):
            moved = pltpu.roll(x, back, 1)
            head = jnp.where(t8 < back, pltpu.roll(prev, back, 1), moved[:, :SUBLANES, :])
            moved = head if sl == SUBLANES else jnp.concatenate([head, moved[:, SUBLANES:, :]], axis=1)
            acc = acc + moved * w_ref[taps - 1 - back:taps - back, col:col + width]
        prev_buf[:, :, col:col + width] = x[:, tail_lo:tail_lo + SUBLANES, :]
        return acc

    def conv_q(col):
        width = GDN_HEADS * GDN_DK
        x = z_scr[:, O_QKV + col:O_QKV + col + width].reshape(nb, sl, width)
        act = _silu(causal_conv(x, xq_buf, cqw_ref, col, width).reshape(rows, width))
        if col == V_COL:
            qkv_scr[:, col:col + width] = act
            return
        scale = GDN_DK ** -0.5 if col == Q_COL else 1.0
        for h in range(GDN_HEADS):
            ah = act[:, GDN_DK * h:GDN_DK * (h + 1)]
            qkv_scr[:, col + GDN_DK * h:col + GDN_DK * (h + 1)] = (
                ah * (lax.rsqrt(jnp.sum(ah * ah, axis=-1, keepdims=True) + EPS) * scale))

    def conv_a():
        x = (z_scr[:, O_C:O_H] * z_scr[:, O_H:O_QKV]).reshape(nb, sl, CONV_CH)
        acc = causal_conv(x, xa_buf, caw_ref, 0, CONV_CH)
        mix_store(0, rows, 0, CONV_CH, z_scr[:, O_B:O_C] * acc.reshape(rows, CONV_CH))

    def gates():
        zab = z_scr[:, O_AB:IN_W_PAD]
        g = -jnp.exp(alog_ref[...]) * _softplus(zab + dt_ref[...])
        beta = jax.nn.sigmoid(zab)
        lane = lax.broadcasted_iota(jnp.int32, (rows, LANES), 1)
        gb = jnp.where(lane < GDN_HEADS, g, beta)
        if cfg.valid_len < sl:
            t_in_seq = lax.broadcasted_iota(jnp.int32, (rows, LANES), 0) & (sl - 1)
            gb = jnp.where(t_in_seq < cfg.valid_len, gb, 0.0)
        gb_scr[...] = gb
        ri = lax.broadcasted_iota(jnp.int32, (rows, rows), 0)
        rj = lax.broadcasted_iota(jnp.int32, (rows, rows), 1)
        lc = _log2(cfg.chunk)
        tri = jnp.where(((ri >> lc) == (rj >> lc)) & (rj <= ri), 1.0, 0.0).astype(BF16)
        g1 = _bf(gb)
        r1 = gb - g1.astype(F32)
        g2 = _bf(r1)
        g3 = _bf(r1 - g2.astype(F32))
        gc_scr[...] = _dot(tri, g1) + (_dot(tri, g2) + _dot(tri, g3))

    def gated_norm(lo, hi):
        for h in range(GDN_HEADS):
            oh = o_scr[lo:hi, GDN_DV * h:GDN_DV * (h + 1)]
            gate = z_scr[lo:hi, O_G + GDN_DV * h:O_G + GDN_DV * (h + 1)]
            mix_store(lo, hi, CONV_CH + GDN_DV * h, CONV_CH + GDN_DV * (h + 1),
                      _rms(oh, gnorm_ref[...]) * _silu(gate))

    def project(src_ref, groups):
        if "xn" not in st:
            st["xn"] = _bf(_rms(padded_rows(src_ref), nmix_ref[...]))
        for group in groups:
            if group == 0:
                proj(O_QKV, O_G)
            elif group == 1:
                proj(O_B, O_QKV)
            else:
                proj(O_G, O_AB)
                z_ab = _dot(st["xn"], win_ref[:, O_AB:O_AB + 2 * GDN_HEADS])
                z_scr[:, O_AB:] = jnp.pad(z_ab, ((0, 0), (0, LANES - 2 * GDN_HEADS)))

    def front_half():
        project(x_ref, (0, 1))
        yield
        conv_q(Q_COL)
        yield
        project(x_ref, (2,))
        yield
        conv_q(K_COL)
        yield
        conv_q(V_COL)
        yield
        conv_a()
        yield
        gates()
        yield
        for lo in range(0, len(problems), PREP_GROUP):
            for _ in _delta_prep(cfg, problems[lo:lo + PREP_GROUP], qkv_scr, gb_scr, gc_scr, st):
                yield
        per = cfg.chunk if cfg.carry else 2 * (VC_ROWS // GDN_HEADS)
        done = len(problems) * per
        if done < rows:
            o_scr[done:, :] = jnp.zeros((rows - done, GDN_HEADS * GDN_DV), F32)
            gated_norm(done, rows)
        yield
        for i in range(len(problems)):
            if cfg.carry:
                _delta_apply_carry(cfg, problems[i:i + 1], st["preps"][i:i + 1], s_scr, o_scr)
            else:
                yield from _delta_apply_blocks(cfg, problems[i:i + 1], st["preps"][i:i + 1], ssm_ref, nssm_ref, o_scr)
            gated_norm(i * per, (i + 1) * per)
            yield

    def real_rows(v):
        g = v.reshape(v.shape[0] // (2 * sl), 2 * sl, v.shape[1])
        t8 = lax.broadcasted_iota(jnp.int32, (g.shape[0], sl, g.shape[2]), 1)
        return jnp.where(t8 < cfg.valid_len, g[:, :sl], pltpu.roll(g[:, sl:], cfg.valid_len, 1)).reshape(
            v.shape[0] // 2, v.shape[1])

    def padded_seqs(v):
        g = v.reshape(v.shape[0] // sl, sl, v.shape[1])
        return jnp.concatenate([g, pltpu.roll(g, cfg.valid_len, 1)], axis=1).reshape(
            v.shape[0] // cfg.valid_len, sl, v.shape[1])

    def back_half():
        if grouped:
            h1 = real_rows(padded_rows(xb_ref)) + _dot(_bf(real_rows(mix_scr[...])), wout_ref[...])
        else:
            h1 = padded_rows(xb_ref) + _dot(mix_scr[...], wout_ref[...])
        hn = _bf(_rms(h1, nffn_ref[...]))
        yield
        for c in range(FFN_HIDDEN // FFN_CHUNK):
            up = jnp.maximum(_dot(hn, wup_ref[:, c * FFN_CHUNK:(c + 1) * FFN_CHUNK]), 0.0)
            up_scr[:, c * FFN_CHUNK:(c + 1) * FFN_CHUNK] = _bf(up * up)
            yield
        outs = []
        for n in range(D_MODEL // FFN_CHUNK):
            cols = slice(n * FFN_CHUNK, (n + 1) * FFN_CHUNK)
            outs.append(h1[:, cols] + _dot(up_scr[...], wdown_ref[:, cols]))
            yield
        h2 = jnp.concatenate(outs, axis=1)
        y = _rms(h2, nfin_ref[...])
        y = padded_seqs(y) if grouped else y.reshape(nb, sl, D_MODEL)
        y_ref[...] = y[:, :cfg.x_len, :]

    back = back_half()
    if cfg.states_only:
        back = iter(())
        y_ref[...] = jnp.zeros(y_ref.shape, F32)
    if cfg.pipelined:
        next(back)
        for i, _ in enumerate(front_half()):
            for _ in range(cfg.back_after[i] if i < len(cfg.back_after) else 0):
                next(back, None)
    else:
        for _ in front_half():
            pass
    if grouped:
        @pl.when(lax.rem(step, cfg.back_group) == cfg.back_group - 1)
        def _():
            for _ in back:
                pass
    else:
        for _ in back:
            pass

    @pl.when(last)
    def _():
        end = cfg.valid_len - tail_lo
        nta_ref[...] = xa_buf[:, end - (SCONV_K - 1):end, :]
        ntq_ref[...] = xq_buf[:, end - (GDN_CONV_K - 1):end, :]
        if cfg.carry:
            nssm_ref[0] = s_scr[...]


def _layer_call(cfg, name, x, tail_a, tail_q, ssm, params):
    nb, sl, rows = cfg.nb, cfg.seq_len, cfg.rows
    assert x.shape == (cfg.batch, cfg.x_len * cfg.n_tt, D_MODEL), (x.shape, cfg)
    assert cfg.batch % nb == 0 and sl % SUBLANES == 0 and rows % VC_ROWS == 0
    assert not cfg.shared_state or nb == 1
    n_tiles, n_tt = cfg.n_tiles, cfg.n_tt

    def front_tile(s):
        return jnp.minimum(s, n_tiles - 1) if cfg.pipelined else s

    def back_tile(s):
        return jnp.maximum(s - 1, 0) if cfg.pipelined else s

    def tile_spec(which):
        return pl.BlockSpec((nb, cfg.x_len, D_MODEL), lambda s: (which(s) // n_tt, which(s) % n_tt, 0))

    group = cfg.back_group
    assert group == 1 or (n_tt == 1 and n_tiles % group == 0)
    back_spec = tile_spec(back_tile) if group == 1 else pl.BlockSpec(
        (nb * group, cfg.x_len, D_MODEL), lambda s: (s // group, 0, 0))

    def state_spec(shape):
        zeros = (0,) * len(shape)
        if cfg.shared_state:
            return pl.BlockSpec((nb,) + shape, lambda s: (0,) + zeros)
        return pl.BlockSpec((nb,) + shape, lambda s: (front_tile(s) // n_tt,) + zeros)

    def out_state_spec(shape):
        zeros = (0,) * len(shape)
        return pl.BlockSpec((nb,) + shape, lambda s: (front_tile(s) // n_tt,) + zeros)

    def const_spec(arr):
        zeros = (0,) * arr.ndim
        return pl.BlockSpec(arr.shape, lambda s: zeros, pipeline_mode=pl.Buffered(1))

    ta_shape, tq_shape = (SCONV_K - 1, CONV_CH), (GDN_CONV_K - 1, QKV_W)
    ssm_shape = (GDN_HEADS, GDN_DK, GDN_DV)
    in_specs = [tile_spec(front_tile), back_spec,
                state_spec(ta_shape), state_spec(tq_shape), state_spec(ssm_shape)]
    in_specs += [pl.BlockSpec(memory_space=pl.ANY) if cfg.states_only and i >= FIRST_BACK_PARAM else const_spec(p)
                 for i, p in enumerate(params)]
    out_specs = [back_spec, out_state_spec(ta_shape), out_state_spec(tq_shape), out_state_spec(ssm_shape)]
    out_shape = [
        jax.ShapeDtypeStruct(x.shape, F32),
        jax.ShapeDtypeStruct((cfg.batch,) + ta_shape, F32),
        jax.ShapeDtypeStruct((cfg.batch,) + tq_shape, F32),
        jax.ShapeDtypeStruct((cfg.batch,) + ssm_shape, F32),
    ]
    scratch = [
        pltpu.VMEM((rows, IN_W_PAD), F32),
        pltpu.VMEM((nb, SUBLANES, CONV_CH), F32),
        pltpu.VMEM((nb, SUBLANES, QKV_W), F32),
        pltpu.VMEM((rows, QKV_W), F32),
        pltpu.VMEM((rows, LANES), F32),
        pltpu.VMEM((rows, LANES), F32),
        pltpu.VMEM((rows, GDN_HEADS * GDN_DV), F32),
        pltpu.VMEM(ssm_shape, F32),
        pltpu.VMEM((rows * group, D_MODEL), BF16 if group == 1 else F32),
        pltpu.VMEM((rows if group == 1 else rows * group * cfg.valid_len // sl, FFN_HIDDEN), BF16),
    ]
    return pl.pallas_call(
        functools.partial(_layer_kernel, cfg),
        grid=(n_tiles + (1 if cfg.pipelined else 0),),
        in_specs=in_specs,
        out_specs=out_specs,
        out_shape=out_shape,
        scratch_shapes=scratch,
        compiler_params=pltpu.CompilerParams(
            dimension_semantics=("arbitrary",), vmem_limit_bytes=VMEM_LIMIT_BYTES),
        name=name,
    )(x, x, tail_a, tail_q, ssm, *params)


META_PAD_LEN = 128
FIRST_BACK_PARAM = 7
PROMPT_TILE = 256
SAMPLE_PAD_LEN = 8
SAMPLE_SEQS_PER_STEP = 16


def kernel(x_prompt, x_sample, state_conv_a, state_conv_qkv, state_ssm, meta_tokens, norm_mix, w_in, conv_a_w,
           conv_qkv_w, a_log, dt_bias, gdn_norm, w_out, norm_ffn, w_up, w_down, norm_final):
    assert norm_mix.shape[0] == 1, "single layer"
    bp, tp, _ = x_prompt.shape
    bs, ts, _ = x_sample.shape
    n_meta = meta_tokens.shape[0]

    def row(v):
        return v.reshape(1, -1).astype(F32)

    def lane_row(v):
        return jnp.pad(v.astype(F32), (0, LANES - v.shape[0])).reshape(1, LANES)

    params = (
        row(norm_mix[0]),
        _bf(w_in[0]),
        conv_a_w[0].astype(F32), conv_qkv_w[0].astype(F32),
        lane_row(a_log[0]), lane_row(dt_bias[0]), row(gdn_norm[0]),
        _bf(w_out[0]), row(norm_ffn[0]),
        _bf(w_up[0]),
        _bf(w_down[0]),
        row(norm_final),
    )

    meta_cfg = Cfg(batch=1, nb=1, seq_len=META_PAD_LEN, valid_len=n_meta, chunk=64, n_tt=1, shared_state=False,
                   pipelined=False, states_only=True)
    meta_x = meta_tokens.astype(F32)[None]
    _, ta_m, tq_m, ssm_m = _layer_call(
        meta_cfg, "layer_meta", meta_x,
        jnp.zeros((1, SCONV_K - 1, CONV_CH), F32), jnp.zeros((1, GDN_CONV_K - 1, QKV_W), F32),
        jnp.zeros((1, GDN_HEADS, GDN_DK, GDN_DV), F32), params)

    prompt_cfg = Cfg(batch=bp, nb=1, seq_len=PROMPT_TILE, valid_len=PROMPT_TILE, chunk=64,
                     n_tt=tp // PROMPT_TILE, shared_state=True, pipelined=True, states_only=False,
                     back_after=PROMPT_BACK_AFTER)
    y_p, ta_p, tq_p, ssm_p = _layer_call(prompt_cfg, "layer_prompt", x_prompt, ta_m, tq_m, ssm_m, params)

    sample_cfg = Cfg(batch=bs, nb=SAMPLE_SEQS_PER_STEP, seq_len=SAMPLE_PAD_LEN, valid_len=ts, chunk=SAMPLE_PAD_LEN,
                     n_tt=1, shared_state=False, pipelined=False, states_only=False, back_group=2)
    y_s, ta_s, tq_s, ssm_s = _layer_call(
        sample_cfg, "layer_sample", x_sample, state_conv_a[0], state_conv_qkv[0], state_ssm[0], params)

    return (y_p, y_s, ta_p[None], tq_p[None], ssm_p[None], ta_s[None], tq_s[None], ssm_s[None])
```

```python
import dataclasses
import functools

import jax
import jax.numpy as jnp
from jax import lax
from jax.experimental import pallas as pl
from jax.experimental.pallas import tpu as pltpu

F32 = jnp.float32
BF16 = jnp.bfloat16

D_MODEL = 1024
CONV_CH = 512
SCONV_K = 3
GDN_HEADS = 4
GDN_DK = 128
GDN_DV = 128
GDN_CONV_K = 4
QKV_W = GDN_HEADS * (2 * GDN_DK + GDN_DV)
FFN_HIDDEN = 4 * D_MODEL
EPS = 1e-6

O_B = 0
O_C = O_B + CONV_CH
O_H = O_C + CONV_CH
O_QKV = O_H + CONV_CH
O_G = O_QKV + QKV_W
O_AB = O_G + GDN_HEADS * GDN_DV
LANES = 128
SUBLANES = 8
IN_W_PAD = O_AB + LANES
VC_ROWS = 128
NEUMANN_BASE = 8
FFN_CHUNK = 256
PREP_GROUP = 4
PROMPT_BACK_AFTER = (0, 0, 0, 3, 2, 3, 2, 2, 1, 1, 1, 1, 1, 0, 1, 1, 1, 1)
VMEM_LIMIT_BYTES = 56 * 1024 * 1024

Q_COL = 0
K_COL = GDN_HEADS * GDN_DK
V_COL = 2 * GDN_HEADS * GDN_DK


@dataclasses.dataclass(frozen=True)
class Cfg:
    batch: int
    nb: int
    seq_len: int
    valid_len: int
    chunk: int
    n_tt: int
    shared_state: bool
    pipelined: bool
    states_only: bool
    back_after: tuple = ()
    back_group: int = 1

    @property
    def rows(self):
        return self.nb * self.seq_len

    @property
    def n_tiles(self):
        return (self.batch // self.nb) * self.n_tt

    @property
    def x_len(self):
        return self.valid_len if self.n_tt == 1 else self.seq_len

    @property
    def carry(self):
        return self.nb == 1


def _bf(x):
    return x.astype(BF16)


def _dot(a, b):
    return jnp.dot(a, b, preferred_element_type=F32)


def _dot_nt(a, b):
    return lax.dot_general(a, b, (((1,), (1,)), ((), ())), preferred_element_type=F32)


def _rms(x, g_row):
    return x * lax.rsqrt(jnp.mean(x * x, axis=-1, keepdims=True) + EPS) * g_row


def _softplus(x):
    return jnp.maximum(x, 0.0) + jnp.log(1.0 + jnp.exp(-jnp.abs(x)))


def _silu(x):
    return x * jax.nn.sigmoid(x)


def _log2(n):
    k = n.bit_length() - 1
    assert (1 << k) == n, n
    return k


def _halves(x):
    return x[:, :LANES], x[:, LANES:]


def _pair(a, b):
    return jnp.concatenate([a, b], axis=1)


def _block_diag(y):
    ya, yb = _halves(y)
    z = jnp.zeros_like(ya)
    return jnp.concatenate([_pair(ya, z), _pair(z, yb)], axis=0)


def _pmm(x, y):
    return _dot(_bf(x), _block_diag(_bf(y)))


def _each(f, *lists):
    return [f(*xs) for xs in zip(*lists)]


def _problems(cfg):
    c = cfg.chunk
    if cfg.carry:
        assert 2 * c == VC_ROWS and GDN_HEADS == 4
        n = -(-cfg.valid_len // c)
        return [([(u * c, c, 0), (u * c, c, 2)], [(u * c, c, 1), (u * c, c, 3)]) for u in range(n)]
    half = VC_ROWS // GDN_HEADS
    assert half % cfg.seq_len == 0 and cfg.seq_len == c and cfg.rows % (2 * half) == 0
    return [([(u * 2 * half, half, h) for h in range(GDN_HEADS)],
             [(u * 2 * half + half, half, h) for h in range(GDN_HEADS)]) for u in range(cfg.rows // (2 * half))]


@dataclasses.dataclass
class Prep:
    u: jax.Array
    w: jax.Array
    qk: jax.Array
    q_dec: jax.Array
    k_dec: jax.Array
    g_last: tuple


def _delta_prep(cfg, problems, qkv_scr, gb_scr, gc_scr, out):
    c = cfg.chunk
    lc = _log2(c)
    ii = lax.broadcasted_iota(jnp.int32, (VC_ROWS, 2 * LANES), 0)
    jj = lax.broadcasted_iota(jnp.int32, (VC_ROWS, 2 * LANES), 1) & (LANES - 1)
    same = (ii >> lc) == (jj >> lc)
    lower = same & (jj <= ii)
    strict = same & (jj < ii)
    end_col = (jj == (ii | (c - 1)))[:, :LANES]
    eye = jnp.where(ii == jj, 1.0, 0.0)

    def stacked(ref, col, width, half):
        return jnp.concatenate([ref[r0:r0 + n, col(h):col(h) + width] for (r0, n, h) in half], axis=0)

    def packed(ref, base):
        return [_pair(*[stacked(ref, lambda h: base + LANES * h, LANES, half) for half in pr]) for pr in problems]

    def columns(ref, base):
        return [[stacked(ref, lambda h: base + h, 1, half) for half in pr] for pr in problems]

    def spread(cols):
        return _pair(*[jnp.broadcast_to(col, (VC_ROWS, LANES)) for col in cols])

    q, k, v = packed(qkv_scr, Q_COL), packed(qkv_scr, K_COL), packed(qkv_scr, V_COL)
    beta_c, gcum_c = columns(gb_scr, GDN_HEADS), columns(gc_scr, 0)
    beta, gcum = _each(spread, beta_c), _each(spread, gcum_c)
    g_rows = _each(lambda g: _pair(*[h.T for h in _halves(g)]), gcum)
    decay = _each(lambda g, gr: jnp.where(lower, jnp.exp(g - gr), 0.0), gcum, g_rows)
    g_last = _each(lambda gr: tuple(jnp.sum(jnp.where(end_col, h, 0.0), axis=1, keepdims=True) for h in _halves(gr)),
                   g_rows)
    kq = _each(lambda kk, qq: _dot_nt(_bf(jnp.concatenate([kk, qq], axis=0)), _block_diag(_bf(kk))), k, q)
    qk = _each(lambda m, d: m[VC_ROWS:] * d, kq, decay)
    a = _each(lambda m, b, d: jnp.where(strict, m[:VC_ROWS] * b * d, 0.0), kq, beta, decay)
    yield

    base = min(c, NEUMANN_BASE)
    lb = _log2(base)
    p = _each(lambda x: jnp.where((ii >> lb) == (jj >> lb), x, 0.0), a)
    t = _each(lambda x: eye - x, p)
    for _ in range(lb - 1):
        p = _each(_pmm, p, p)
        t = _each(lambda x, y: x + _pmm(x, y), t, p)
        yield
    size = base
    while size < c:
        ls = _log2(size)
        couple = ((ii >> (ls + 1)) == (jj >> (ls + 1))) & ((ii >> ls) != (jj >> ls))
        off = _each(lambda x: jnp.where(couple, x, 0.0), a)
        t_off = _each(_pmm, t, off)
        t = _each(lambda x, y: x - _pmm(y, x), t, t_off)
        size *= 2
        yield

    exp_g = _each(lambda cols: spread([jnp.exp(col) for col in cols]), gcum_c)
    k_scale = _each(lambda cols, gl: spread([jnp.exp(e - col) for col, e in zip(cols, gl)]), gcum_c, g_last)
    u = _each(lambda x, vv, b: _pmm(x, vv * b), t, v, beta)
    w = _each(lambda x, kk, b, e: _pmm(x, kk * (b * e)), t, k, beta, exp_g)
    q_dec = _each(lambda x, e: x * e, q, exp_g)
    k_dec = _each(lambda x, s: x * s, k, k_scale)
    out.setdefault("preps", []).extend(Prep(*xs) for xs in zip(u, w, qk, q_dec, k_dec, g_last))


def _delta_apply_carry(cfg, problems, preps, s_scr, o_scr):
    c = cfg.chunk
    top_rows = lax.broadcasted_iota(jnp.int32, (VC_ROWS, GDN_DV), 0) < c
    for (half_a, half_b), pr in zip(problems, preps):
        state = [s_scr[h] for h in range(GDN_HEADS)]
        w_s, q_s = [], []
        for blk in range(VC_ROWS // c):
            lo = blk * c
            wq = _bf(jnp.concatenate([pr.w[lo:lo + c], pr.q_dec[lo:lo + c]], axis=0))
            r = _dot(wq, _block_diag(_bf(_pair(state[half_a[blk][2]], state[half_b[blk][2]]))))
            w_s.append(r[:c])
            q_s.append(r[c:])
        v_new = pr.u - jnp.concatenate(w_s, axis=0)
        o = jnp.concatenate(q_s, axis=0) + _pmm(pr.qk, v_new)
        for half, vn, kd, gl, o_h in zip((half_a, half_b), _halves(v_new), _halves(pr.k_dec), pr.g_last, _halves(o)):
            upd = _dot(_bf(kd.T), _bf(_pair(jnp.where(top_rows, vn, 0.0), jnp.where(top_rows, 0.0, vn))))
            for blk, (r0, n, h) in enumerate(half):
                lo = blk * c
                s_scr[h] = state[h] * jnp.exp(gl[lo:lo + 1, :]) + upd[:, GDN_DV * blk:GDN_DV * (blk + 1)]
                o_scr[r0:r0 + n, GDN_DV * h:GDN_DV * (h + 1)] = o_h[lo:lo + c]


def _delta_apply_blocks(cfg, problems, preps, ssm_ref, nssm_ref, o_scr):
    c = cfg.chunk
    lc = _log2(c)
    blk_id = lax.broadcasted_iota(jnp.int32, (VC_ROWS, GDN_DV), 0) >> lc
    for pieces, pr in zip(problems, preps):
        halves = []
        for half, w_h, qd_h, u_h in zip(pieces, _halves(pr.w), _halves(pr.q_dec), _halves(pr.u)):
            blocks = [(r0 + t0, h, (r0 + t0) // cfg.seq_len) for (r0, n, h) in half for t0 in range(0, n, c)]
            vs, qs = [], []
            for b, (_, h, seq) in enumerate(blocks):
                lo = b * c
                r = _dot(_bf(jnp.concatenate([w_h[lo:lo + c], qd_h[lo:lo + c]], axis=0)), _bf(ssm_ref[seq, h]))
                vs.append(u_h[lo:lo + c] - r[:c])
                qs.append(r[c:])
            halves.append((blocks, jnp.concatenate(vs, axis=0), jnp.concatenate(qs, axis=0)))
        v_new = _pair(halves[0][1], halves[1][1])
        o = _pair(halves[0][2], halves[1][2]) + _pmm(pr.qk, v_new)
        for (blocks, vn, _), kd, gl, o_h in zip(halves, _halves(pr.k_dec), pr.g_last, _halves(o)):
            kd_t = _bf(kd.T)
            for b0 in range(0, len(blocks), 2):
                upd = _dot(kd_t, _bf(_pair(jnp.where(blk_id == b0, vn, 0.0), jnp.where(blk_id == b0 + 1, vn, 0.0))))
                for b in (b0, b0 + 1):
                    row, h, seq = blocks[b]
                    lo = b * c
                    nssm_ref[seq, h] = (ssm_ref[seq, h] * jnp.exp(gl[lo:lo + 1, :])
                                        + upd[:, GDN_DV * (b - b0):GDN_DV * (b - b0 + 1)])
                    o_scr[row:row + c, GDN_DV * h:GDN_DV * (h + 1)] = o_h[lo:lo + c]


def _layer_kernel(cfg, x_ref, xb_ref, ta_ref, tq_ref, ssm_ref, nmix_ref, win_ref, caw_ref, cqw_ref, alog_ref, dt_ref,
                  gnorm_ref, wout_ref, nffn_ref, wup_ref, wdown_ref, nfin_ref,
                  y_ref, nta_ref, ntq_ref, nssm_ref,
                  z_scr, xa_buf, xq_buf, qkv_scr, gb_scr, gc_scr, o_scr, s_scr, mix_scr, up_scr):
    step = pl.program_id(0)
    nb, sl, rows = cfg.nb, cfg.seq_len, cfg.rows
    if cfg.pipelined:
        tile = jnp.minimum(step, cfg.n_tiles - 1)
        real = step < cfg.n_tiles
    else:
        tile = step
        real = True
    j = lax.rem(tile, cfg.n_tt)
    first = j == 0
    last = (j == cfg.n_tt - 1) & real

    @pl.when(first)
    def _():
        xa_buf[...] = jnp.zeros(xa_buf.shape, F32)
        xq_buf[...] = jnp.zeros(xq_buf.shape, F32)
        xa_buf[:, SUBLANES - (SCONV_K - 1):, :] = ta_ref[...]
        xq_buf[:, SUBLANES - (GDN_CONV_K - 1):, :] = tq_ref[...]
        if cfg.carry:
            s_scr[...] = ssm_ref[0]

    st = {}
    problems = _problems(cfg)

    def padded_rows(ref):
        x = ref[...]
        if cfg.x_len < sl:
            x = jnp.concatenate([x, jnp.zeros((x.shape[0], sl - cfg.x_len, D_MODEL), F32)], axis=1)
        return x.reshape(x.shape[0] * sl, D_MODEL)

    grouped = cfg.back_group > 1
    if grouped:
        assert not cfg.pipelined and not cfg.states_only and sl == SUBLANES and 2 * cfg.valid_len == sl
        mix_base = pl.multiple_of(lax.rem(step, cfg.back_group) * rows, rows)

    def mix_store(lo, hi, col_lo, col_hi, value):
        if grouped:
            mix_scr[pl.ds(mix_base + lo, hi - lo), col_lo:col_hi] = value
        else:
            mix_scr[lo:hi, col_lo:col_hi] = _bf(value)

    def proj(lo, hi):
        z_scr[:, lo:hi] = _dot(st["xn"], win_ref[:, lo:hi])

    if cfg.pipelined:
        @pl.when(step == 0)
        def _():
            mix_scr[...] = jnp.zeros(mix_scr.shape, mix_scr.dtype)

    tail_lo = (cfg.valid_len - 1) // SUBLANES * SUBLANES
    assert cfg.valid_len - tail_lo >= GDN_CONV_K - 1 and (cfg.n_tt == 1 or cfg.valid_len == sl)

    def causal_conv(x, prev_buf, w_ref, col, width):
        taps = w_ref.shape[0]
        t8 = lax.broadcasted_iota(jnp.int32, (nb, SUBLANES, width), 1)
        prev = prev_buf[:, :, col:col + width]
        acc = x * w_ref[taps - 1:taps, col:col + width]
        for back in range(1, taps):
            moved = pltpu.roll(x, back, 1)
            head = jnp.where(t8 < back, pltpu.roll(prev, back, 1), moved[:, :SUBLANES, :])
            moved = head if sl == SUBLANES else jnp.concatenate([head, moved[:, SUBLANES:, :]], axis=1)
            acc = acc + moved * w_ref[taps - 1 - back:taps - back, col:col + width]
        prev_buf[:, :, col:col + width] = x[:, tail_lo:tail_lo + SUBLANES, :]
        return acc

    def conv_q(col):
        width = GDN_HEADS * GDN_DK
        x = z_scr[:, O_QKV + col:O_QKV + col + width].reshape(nb, sl, width)
        act = _silu(causal_conv(x, xq_buf, cqw_ref, col, width).reshape(rows, width))
        if col == V_COL:
            qkv_scr[:, col:col + width] = act
            return
        scale = GDN_DK ** -0.5 if col == Q_COL else 1.0
        for h in range(GDN_HEADS):
            ah = act[:, GDN_DK * h:GDN_DK * (h + 1)]
            qkv_scr[:, col + GDN_DK * h:col + GDN_DK * (h + 1)] = (
                ah * (lax.rsqrt(jnp.sum(ah * ah, axis=-1, keepdims=True) + EPS) * scale))

    def conv_a():
        x = (z_scr[:, O_C:O_H] * z_scr[:, O_H:O_QKV]).reshape(nb, sl, CONV_CH)
        acc = causal_conv(x, xa_buf, caw_ref, 0, CONV_CH)
        mix_store(0, rows, 0, CONV_CH, z_scr[:, O_B:O_C] * acc.reshape(rows, CONV_CH))

    def gates():
        zab = z_scr[:, O_AB:IN_W_PAD]
        g = -jnp.exp(alog_ref[...]) * _softplus(zab + dt_ref[...])
        beta = jax.nn.sigmoid(zab)
        lane = lax.broadcasted_iota(jnp.int32, (rows, LANES), 1)
        gb = jnp.where(lane < GDN_HEADS, g, beta)
        if cfg.valid_len < sl:
            t_in_seq = lax.broadcasted_iota(jnp.int32, (rows, LANES), 0) & (sl - 1)
            gb = jnp.where(t_in_seq < cfg.valid_len, gb, 0.0)
        gb_scr[...] = gb
        ri = lax.broadcasted_iota(jnp.int32, (rows, rows), 0)
        rj = lax.broadcasted_iota(jnp.int32, (rows, rows), 1)
        lc = _log2(cfg.chunk)
        tri = jnp.where(((ri >> lc) == (rj >> lc)) & (rj <= ri), 1.0, 0.0).astype(BF16)
        g1 = _bf(gb)
        r1 = gb - g1.astype(F32)
        g2 = _bf(r1)
        g3 = _bf(r1 - g2.astype(F32))
        gc_scr[...] = _dot(tri, g1) + (_dot(tri, g2) + _dot(tri, g3))

    def gated_norm(lo, hi):
        for h in range(GDN_HEADS):
            oh = o_scr[lo:hi, GDN_DV * h:GDN_DV * (h + 1)]
            gate = z_scr[lo:hi, O_G + GDN_DV * h:O_G + GDN_DV * (h + 1)]
            mix_store(lo, hi, CONV_CH + GDN_DV * h, CONV_CH + GDN_DV * (h + 1),
                      _rms(oh, gnorm_ref[...]) * _silu(gate))

    def project(src_ref, groups):
        if "xn" not in st:
            st["xn"] = _bf(_rms(padded_rows(src_ref), nmix_ref[...]))
        for group in groups:
            if group == 0:
                proj(O_QKV, O_G)
            elif group == 1:
                proj(O_B, O_QKV)
            else:
                proj(O_G, O_AB)
                z_ab = _dot(st["xn"], win_ref[:, O_AB:O_AB + 2 * GDN_HEADS])
                z_scr[:, O_AB:] = jnp.pad(z_ab, ((0, 0), (0, LANES - 2 * GDN_HEADS)))

    def front_half():
        project(x_ref, (0, 1))
        yield
        conv_q(Q_COL)
        yield
        project(x_ref, (2,))
        yield
        conv_q(K_COL)
        yield
        conv_q(V_COL)
        yield
        conv_a()
        yield
        gates()
        yield
        for lo in range(0, len(problems), PREP_GROUP):
            for _ in _delta_prep(cfg, problems[lo:lo + PREP_GROUP], qkv_scr, gb_scr, gc_scr, st):
                yield
        per = cfg.chunk if cfg.carry else 2 * (VC_ROWS // GDN_HEADS)
        done = len(problems) * per
        if done < rows:
            o_scr[done:, :] = jnp.zeros((rows - done, GDN_HEADS * GDN_DV), F32)
            gated_norm(done, rows)
        yield
        for i in range(len(problems)):
            if cfg.carry:
                _delta_apply_carry(cfg, problems[i:i + 1], st["preps"][i:i + 1], s_scr, o_scr)
            else:
                _delta_apply_blocks(cfg, problems[i:i + 1], st["preps"][i:i + 1], ssm_ref, nssm_ref, o_scr)
            gated_norm(i * per, (i + 1) * per)
            yield

    def real_rows(v):
        g = v.reshape(v.shape[0] // (2 * sl), 2 * sl, v.shape[1])
        t8 = lax.broadcasted_iota(jnp.int32, (g.shape[0], sl, g.shape[2]), 1)
        return jnp.where(t8 < cfg.valid_len, g[:, :sl], pltpu.roll(g[:, sl:], cfg.valid_len, 1)).reshape(
            v.shape[0] // 2, v.shape[1])

    def padded_seqs(v):
        g = v.reshape(v.shape[0] // sl, sl, v.shape[1])
        return jnp.concatenate([g, pltpu.roll(g, cfg.valid_len, 1)], axis=1).reshape(
            v.shape[0] // cfg.valid_len, sl, v.shape[1])

    def back_half():
        if grouped:
            h1 = real_rows(padded_rows(xb_ref)) + _dot(_bf(real_rows(mix_scr[...])), wout_ref[...])
        else:
            h1 = padded_rows(xb_ref) + _dot(mix_scr[...], wout_ref[...])
        hn = _bf(_rms(h1, nffn_ref[...]))
        yield
        for c in range(FFN_HIDDEN // FFN_CHUNK):
            up = jnp.maximum(_dot(hn, wup_ref[:, c * FFN_CHUNK:(c + 1) * FFN_CHUNK]), 0.0)
            up_scr[:, c * FFN_CHUNK:(c + 1) * FFN_CHUNK] = _bf(up * up)
            yield
        outs = []
        for n in range(D_MODEL // FFN_CHUNK):
            cols = slice(n * FFN_CHUNK, (n + 1) * FFN_CHUNK)
            outs.append(h1[:, cols] + _dot(up_scr[...], wdown_ref[:, cols]))
            yield
        h2 = jnp.concatenate(outs, axis=1)
        y = _rms(h2, nfin_ref[...])
        y = padded_seqs(y) if grouped else y.reshape(nb, sl, D_MODEL)
        y_ref[...] = y[:, :cfg.x_len, :]

    back = back_half()
    if cfg.states_only:
        back = iter(())
        y_ref[...] = jnp.zeros(y_ref.shape, F32)
    if cfg.pipelined:
        next(back)
        for i, _ in enumerate(front_half()):
            for _ in range(cfg.back_after[i] if i < len(cfg.back_after) else 0):
                next(back, None)
    else:
        for _ in front_half():
            pass
    if grouped:
        @pl.when(lax.rem(step, cfg.back_group) == cfg.back_group - 1)
        def _():
            for _ in back:
                pass
    else:
        for _ in back:
            pass

    @pl.when(last)
    def _():
        end = cfg.valid_len - tail_lo
        nta_ref[...] = xa_buf[:, end - (SCONV_K - 1):end, :]
        ntq_ref[...] = xq_buf[:, end - (GDN_CONV_K - 1):end, :]
        if cfg.carry:
            nssm_ref[0] = s_scr[...]


def _layer_call(cfg, name, x, tail_a, tail_q, ssm, params):
    nb, sl, rows = cfg.nb, cfg.seq_len, cfg.rows
    assert x.shape == (cfg.batch, cfg.x_len * cfg.n_tt, D_MODEL), (x.shape, cfg)
    assert cfg.batch % nb == 0 and sl % SUBLANES == 0 and rows % VC_ROWS == 0
    assert not cfg.shared_state or nb == 1
    n_tiles, n_tt = cfg.n_tiles, cfg.n_tt

    def front_tile(s):
        return jnp.minimum(s, n_tiles - 1) if cfg.pipelined else s

    def back_tile(s):
        return jnp.maximum(s - 1, 0) if cfg.pipelined else s

    def tile_spec(which):
        return pl.BlockSpec((nb, cfg.x_len, D_MODEL), lambda s: (which(s) // n_tt, which(s) % n_tt, 0))

    group = cfg.back_group
    assert group == 1 or (n_tt == 1 and n_tiles % group == 0)
    back_spec = tile_spec(back_tile) if group == 1 else pl.BlockSpec(
        (nb * group, cfg.x_len, D_MODEL), lambda s: (s // group, 0, 0))

    def state_spec(shape):
        zeros = (0,) * len(shape)
        if cfg.shared_state:
            return pl.BlockSpec((nb,) + shape, lambda s: (0,) + zeros)
        return pl.BlockSpec((nb,) + shape, lambda s: (front_tile(s) // n_tt,) + zeros)

    def out_state_spec(shape):
        zeros = (0,) * len(shape)
        return pl.BlockSpec((nb,) + shape, lambda s: (front_tile(s) // n_tt,) + zeros)

    def const_spec(arr):
        zeros = (0,) * arr.ndim
        return pl.BlockSpec(arr.shape, lambda s: zeros, pipeline_mode=pl.Buffered(1))

    ta_shape, tq_shape = (SCONV_K - 1, CONV_CH), (GDN_CONV_K - 1, QKV_W)
    ssm_shape = (GDN_HEADS, GDN_DK, GDN_DV)
    in_specs = [tile_spec(front_tile), back_spec,
                state_spec(ta_shape), state_spec(tq_shape), state_spec(ssm_shape)]
    in_specs += [pl.BlockSpec(memory_space=pl.ANY) if cfg.states_only and i >= FIRST_BACK_PARAM else const_spec(p)
                 for i, p in enumerate(params)]
    out_specs = [back_spec, out_state_spec(ta_shape), out_state_spec(tq_shape), out_state_spec(ssm_shape)]
    out_shape = [
        jax.ShapeDtypeStruct(x.shape, F32),
        jax.ShapeDtypeStruct((cfg.batch,) + ta_shape, F32),
        jax.ShapeDtypeStruct((cfg.batch,) + tq_shape, F32),
        jax.ShapeDtypeStruct((cfg.batch,) + ssm_shape, F32),
    ]
    scratch = [
        pltpu.VMEM((rows, IN_W_PAD), F32),
        pltpu.VMEM((nb, SUBLANES, CONV_CH), F32),
        pltpu.VMEM((nb, SUBLANES, QKV_W), F32),
        pltpu.VMEM((rows, QKV_W), F32),
        pltpu.VMEM((rows, LANES), F32),
        pltpu.VMEM((rows, LANES), F32),
        pltpu.VMEM((rows, GDN_HEADS * GDN_DV), F32),
        pltpu.VMEM(ssm_shape, F32),
        pltpu.VMEM((rows * group, D_MODEL), BF16 if group == 1 else F32),
        pltpu.VMEM((rows if group == 1 else rows * group * cfg.valid_len // sl, FFN_HIDDEN), BF16),
    ]
    return pl.pallas_call(
        functools.partial(_layer_kernel, cfg),
        grid=(n_tiles + (1 if cfg.pipelined else 0),),
        in_specs=in_specs,
        out_specs=out_specs,
        out_shape=out_shape,
        scratch_shapes=scratch,
        compiler_params=pltpu.CompilerParams(
            dimension_semantics=("arbitrary",), vmem_limit_bytes=VMEM_LIMIT_BYTES),
        name=name,
    )(x, x, tail_a, tail_q, ssm, *params)


META_PAD_LEN = 128
FIRST_BACK_PARAM = 7
PROMPT_TILE = 256
SAMPLE_PAD_LEN = 8
SAMPLE_SEQS_PER_STEP = 16


def kernel(x_prompt, x_sample, state_conv_a, state_conv_qkv, state_ssm, meta_tokens, norm_mix, w_in, conv_a_w,
           conv_qkv_w, a_log, dt_bias, gdn_norm, w_out, norm_ffn, w_up, w_down, norm_final):
    assert norm_mix.shape[0] == 1, "single layer"
    bp, tp, _ = x_prompt.shape
    bs, ts, _ = x_sample.shape
    n_meta = meta_tokens.shape[0]

    def row(v):
        return v.reshape(1, -1).astype(F32)

    def lane_row(v):
        return jnp.pad(v.astype(F32), (0, LANES - v.shape[0])).reshape(1, LANES)

    params = (
        row(norm_mix[0]),
        _bf(w_in[0]),
        conv_a_w[0].astype(F32), conv_qkv_w[0].astype(F32),
        lane_row(a_log[0]), lane_row(dt_bias[0]), row(gdn_norm[0]),
        _bf(w_out[0]), row(norm_ffn[0]),
        _bf(w_up[0]),
        _bf(w_down[0]),
        row(norm_final),
    )

    meta_cfg = Cfg(batch=1, nb=1, seq_len=META_PAD_LEN, valid_len=n_meta, chunk=64, n_tt=1, shared_state=False,
                   pipelined=False, states_only=True)
    meta_x = meta_tokens.astype(F32)[None]
    _, ta_m, tq_m, ssm_m = _layer_call(
        meta_cfg, "layer_meta", meta_x,
        jnp.zeros((1, SCONV_K - 1, CONV_CH), F32), jnp.zeros((1, GDN_CONV_K - 1, QKV_W), F32),
        jnp.zeros((1, GDN_HEADS, GDN_DK, GDN_DV), F32), params)

    prompt_cfg = Cfg(batch=bp, nb=1, seq_len=PROMPT_TILE, valid_len=PROMPT_TILE, chunk=64,
                     n_tt=tp // PROMPT_TILE, shared_state=True, pipelined=True, states_only=False,
                     back_after=PROMPT_BACK_AFTER)
    y_p, ta_p, tq_p, ssm_p = _layer_call(prompt_cfg, "layer_prompt", x_prompt, ta_m, tq_m, ssm_m, params)

    sample_cfg = Cfg(batch=bs, nb=SAMPLE_SEQS_PER_STEP, seq_len=SAMPLE_PAD_LEN, valid_len=ts, chunk=SAMPLE_PAD_LEN,
                     n_tt=1, shared_state=False, pipelined=False, states_only=False, back_group=2)
    y_s, ta_s, tq_s, ssm_s = _layer_call(
        sample_cfg, "layer_sample", x_sample, state_conv_a[0], state_conv_qkv[0], state_ssm[0], params)

    return (y_p, y_s, ta_p[None], tq_p[None], ssm_p[None], ta_s[None], tq_s[None], ssm_s[None])
```

```python
import dataclasses
import functools

import jax
import jax.numpy as jnp
from jax import lax
from jax.experimental import pallas as pl
from jax.experimental.pallas import tpu as pltpu

F32 = jnp.float32
BF16 = jnp.bfloat16

D_MODEL = 1024
CONV_CH = 512
SCONV_K = 3
GDN_HEADS = 4
GDN_DK = 128
GDN_DV = 128
GDN_CONV_K = 4
QKV_W = GDN_HEADS * (2 * GDN_DK + GDN_DV)
FFN_HIDDEN = 4 * D_MODEL
EPS = 1e-6

O_B = 0
O_C = O_B + CONV_CH
O_H = O_C + CONV_CH
O_QKV = O_H + CONV_CH
O_G = O_QKV + QKV_W
O_AB = O_G + GDN_HEADS * GDN_DV
LANES = 128
SUBLANES = 8
IN_W_PAD = O_AB + LANES
VC_ROWS = 128
NEUMANN_BASE = 8
FFN_CHUNK = 256
PREP_GROUP = 4
PROMPT_BACK_AFTER = (0, 0, 0, 3, 2, 3, 2, 2, 1, 1, 1, 1, 1, 0, 1, 1, 1, 1)
VMEM_LIMIT_BYTES = 56 * 1024 * 1024

Q_COL = 0
K_COL = GDN_HEADS * GDN_DK
V_COL = 2 * GDN_HEADS * GDN_DK


@dataclasses.dataclass(frozen=True)
class Cfg:
    batch: int
    nb: int
    seq_len: int
    valid_len: int
    chunk: int
    n_tt: int
    shared_state: bool
    pipelined: bool
    states_only: bool
    back_after: tuple = ()
    back_group: int = 1

    @property
    def rows(self):
        return self.nb * self.seq_len

    @property
    def n_tiles(self):
        return (self.batch // self.nb) * self.n_tt

    @property
    def x_len(self):
        return self.valid_len if self.n_tt == 1 else self.seq_len

    @property
    def carry(self):
        return self.nb == 1


def _bf(x):
    return x.astype(BF16)


def _dot(a, b):
    return jnp.dot(a, b, preferred_element_type=F32)


def _dot_nt(a, b):
    return lax.dot_general(a, b, (((1,), (1,)), ((), ())), preferred_element_type=F32)


def _rms(x, g_row):
    return x * lax.rsqrt(jnp.mean(x * x, axis=-1, keepdims=True) + EPS) * g_row


def _softplus(x):
    return jnp.maximum(x, 0.0) + jnp.log(1.0 + jnp.exp(-jnp.abs(x)))


def _silu(x):
    return x * jax.nn.sigmoid(x)


def _log2(n):
    k = n.bit_length() - 1
    assert (1 << k) == n, n
    return k


def _halves(x):
    return x[:, :LANES], x[:, LANES:]


def _pair(a, b):
    return jnp.concatenate([a, b], axis=1)


def _block_diag(y):
    ya, yb = _halves(y)
    z = jnp.zeros_like(ya)
    return jnp.concatenate([_pair(ya, z), _pair(z, yb)], axis=0)


def _pmm(x, y):
    return _dot(_bf(x), _block_diag(_bf(y)))


def _each(f, *lists):
    return [f(*xs) for xs in zip(*lists)]


def _problems(cfg):
    c = cfg.chunk
    if cfg.carry:
        assert 2 * c == VC_ROWS and GDN_HEADS == 4
        n = -(-cfg.valid_len // c)
        return [([(u * c, c, 0), (u * c, c, 2)], [(u * c, c, 1), (u * c, c, 3)]) for u in range(n)]
    half = VC_ROWS // GDN_HEADS
    assert half % cfg.seq_len == 0 and cfg.seq_len == c and cfg.rows % (2 * half) == 0
    return [([(u * 2 * half, half, h) for h in range(GDN_HEADS)],
             [(u * 2 * half + half, half, h) for h in range(GDN_HEADS)]) for u in range(cfg.rows // (2 * half))]


@dataclasses.dataclass
class Prep:
    u: jax.Array
    w: jax.Array
    qk: jax.Array
    q_dec: jax.Array
    k_dec: jax.Array
    g_last: tuple


def _delta_prep(cfg, problems, qkv_scr, gb_scr, gc_scr, out):
    c = cfg.chunk
    lc = _log2(c)
    ii = lax.broadcasted_iota(jnp.int32, (VC_ROWS, 2 * LANES), 0)
    jj = lax.broadcasted_iota(jnp.int32, (VC_ROWS, 2 * LANES), 1) & (LANES - 1)
    same = (ii >> lc) == (jj >> lc)
    lower = same & (jj <= ii)
    strict = same & (jj < ii)
    eye = jnp.where(ii == jj, 1.0, 0.0)

    def stacked(ref, col, width, half):
        return jnp.concatenate([ref[r0:r0 + n, col(h):col(h) + width] for (r0, n, h) in half], axis=0)

    def packed(ref, base):
        return [_pair(*[stacked(ref, lambda h: base + LANES * h, LANES, half) for half in pr]) for pr in problems]

    def columns(ref, base):
        return [[stacked(ref, lambda h: base + h, 1, half) for half in pr] for pr in problems]

    def spread(cols):
        return _pair(*[jnp.broadcast_to(col, (VC_ROWS, LANES)) for col in cols])

    q, k, v = packed(qkv_scr, Q_COL), packed(qkv_scr, K_COL), packed(qkv_scr, V_COL)
    beta_c, gcum_c = columns(gb_scr, GDN_HEADS), columns(gc_scr, 0)
    beta, gcum = _each(spread, beta_c), _each(spread, gcum_c)
    g_rows = _each(lambda g: _pair(*[h.T for h in _halves(g)]), gcum)
    decay = _each(lambda g, gr: jnp.where(lower, jnp.exp(g - gr), 0.0), gcum, g_rows)
    g_last = _each(lambda cols: tuple(jnp.concatenate(
        [jnp.broadcast_to(col[lo + c - 1:lo + c, :], (c, 1)) for lo in range(0, VC_ROWS, c)], axis=0) for col in cols),
        gcum_c)
    kq = _each(lambda kk, qq: _dot_nt(_bf(jnp.concatenate([kk, qq], axis=0)), _block_diag(_bf(kk))), k, q)
    qk = _each(lambda m, d: m[VC_ROWS:] * d, kq, decay)
    a = _each(lambda m, b, d: jnp.where(strict, m[:VC_ROWS] * b * d, 0.0), kq, beta, decay)
    yield

    base = min(c, NEUMANN_BASE)
    lb = _log2(base)
    p = _each(lambda x: jnp.where((ii >> lb) == (jj >> lb), x, 0.0), a)
    t = _each(lambda x: eye - x, p)
    for _ in range(lb - 1):
        p = _each(_pmm, p, p)
        t = _each(lambda x, y: x + _pmm(x, y), t, p)
        yield
    size = base
    while size < c:
        ls = _log2(size)
        couple = ((ii >> (ls + 1)) == (jj >> (ls + 1))) & ((ii >> ls) != (jj >> ls))
        off = _each(lambda x: jnp.where(couple, x, 0.0), a)
        t_off = _each(_pmm, t, off)
        t = _each(lambda x, y: x - _pmm(y, x), t, t_off)
        size *= 2
        yield

    exp_g = _each(lambda cols: spread([jnp.exp(col) for col in cols]), gcum_c)
    k_scale = _each(lambda cols, gl: spread([jnp.exp(e - col) for col, e in zip(cols, gl)]), gcum_c, g_last)
    u = _each(lambda x, vv, b: _pmm(x, vv * b), t, v, beta)
    w = _each(lambda x, kk, b, e: _pmm(x, kk * (b * e)), t, k, beta, exp_g)
    q_dec = _each(lambda x, e: x * e, q, exp_g)
    k_dec = _each(lambda x, s: x * s, k, k_scale)
    out.setdefault("preps", []).extend(Prep(*xs) for xs in zip(u, w, qk, q_dec, k_dec, g_last))


def _delta_apply_carry(cfg, problems, preps, s_scr, o_scr):
    c = cfg.chunk
    top_rows = lax.broadcasted_iota(jnp.int32, (VC_ROWS, GDN_DV), 0) < c
    for (half_a, half_b), pr in zip(problems, preps):
        state = [s_scr[h] for h in range(GDN_HEADS)]
        w_s, q_s = [], []
        for blk in range(VC_ROWS // c):
            lo = blk * c
            wq = _bf(jnp.concatenate([pr.w[lo:lo + c], pr.q_dec[lo:lo + c]], axis=0))
            r = _dot(wq, _block_diag(_bf(_pair(state[half_a[blk][2]], state[half_b[blk][2]]))))
            w_s.append(r[:c])
            q_s.append(r[c:])
        v_new = pr.u - jnp.concatenate(w_s, axis=0)
        o = jnp.concatenate(q_s, axis=0) + _pmm(pr.qk, v_new)
        for half, vn, kd, gl, o_h in zip((half_a, half_b), _halves(v_new), _halves(pr.k_dec), pr.g_last, _halves(o)):
            upd = _dot(_bf(kd.T), _bf(_pair(jnp.where(top_rows, vn, 0.0), jnp.where(top_rows, 0.0, vn))))
            for blk, (r0, n, h) in enumerate(half):
                lo = blk * c
                s_scr[h] = state[h] * jnp.exp(gl[lo:lo + 1, :]) + upd[:, GDN_DV * blk:GDN_DV * (blk + 1)]
                o_scr[r0:r0 + n, GDN_DV * h:GDN_DV * (h + 1)] = o_h[lo:lo + c]


def _delta_apply_blocks(cfg, problems, preps, ssm_ref, nssm_ref, o_scr):
    c = cfg.chunk
    lc = _log2(c)
    blk_id = lax.broadcasted_iota(jnp.int32, (VC_ROWS, GDN_DV), 0) >> lc
    for pieces, pr in zip(problems, preps):
        halves = []
        for half, w_h, qd_h, u_h in zip(pieces, _halves(pr.w), _halves(pr.q_dec), _halves(pr.u)):
            blocks = [(r0 + t0, h, (r0 + t0) // cfg.seq_len) for (r0, n, h) in half for t0 in range(0, n, c)]
            vs, qs = [], []
            for b, (_, h, seq) in enumerate(blocks):
                lo = b * c
                r = _dot(_bf(jnp.concatenate([w_h[lo:lo + c], qd_h[lo:lo + c]], axis=0)), _bf(ssm_ref[seq, h]))
                vs.append(u_h[lo:lo + c] - r[:c])
                qs.append(r[c:])
            halves.append((blocks, jnp.concatenate(vs, axis=0), jnp.concatenate(qs, axis=0)))
        v_new = _pair(halves[0][1], halves[1][1])
        o = _pair(halves[0][2], halves[1][2]) + _pmm(pr.qk, v_new)
        for (blocks, vn, _), kd, gl, o_h in zip(halves, _halves(pr.k_dec), pr.g_last, _halves(o)):
            kd_t = _bf(kd.T)
            for b0 in range(0, len(blocks), 2):
                upd = _dot(kd_t, _bf(_pair(jnp.where(blk_id == b0, vn, 0.0), jnp.where(blk_id == b0 + 1, vn, 0.0))))
                for b in (b0, b0 + 1):
                    row, h, seq = blocks[b]
                    lo = b * c
                    nssm_ref[seq, h] = (ssm_ref[seq, h] * jnp.exp(gl[lo:lo + 1, :])
                                        + upd[:, GDN_DV * (b - b0):GDN_DV * (b - b0 + 1)])
                    o_scr[row:row + c, GDN_DV * h:GDN_DV * (h + 1)] = o_h[lo:lo + c]


def _layer_kernel(cfg, x_ref, xb_ref, ta_ref, tq_ref, ssm_ref, nmix_ref, win_ref, caw_ref, cqw_ref, alog_ref, dt_ref,
                  gnorm_ref, wout_ref, nffn_ref, wup_ref, wdown_ref, nfin_ref,
                  y_ref, nta_ref, ntq_ref, nssm_ref,
                  z_scr, xa_buf, xq_buf, qkv_scr, gb_scr, gc_scr, o_scr, s_scr, mix_scr, up_scr):
    step = pl.program_id(0)
    nb, sl, rows = cfg.nb, cfg.seq_len, cfg.rows
    if cfg.pipelined:
        tile = jnp.minimum(step, cfg.n_tiles - 1)
        real = step < cfg.n_tiles
    else:
        tile = step
        real = True
    j = lax.rem(tile, cfg.n_tt)
    first = j == 0
    last = (j == cfg.n_tt - 1) & real

    @pl.when(first)
    def _():
        xa_buf[...] = jnp.zeros(xa_buf.shape, F32)
        xq_buf[...] = jnp.zeros(xq_buf.shape, F32)
        xa_buf[:, SUBLANES - (SCONV_K - 1):, :] = ta_ref[...]
        for b in range(nb):
            xq_buf[b, SUBLANES - (GDN_CONV_K - 1):, :] = tq_ref[:, b, :]
        if cfg.carry:
            s_scr[...] = ssm_ref[0]

    st = {}
    problems = _problems(cfg)

    def padded_rows(ref):
        x = ref[...]
        if cfg.x_len < sl:
            x = jnp.concatenate([x, jnp.zeros((x.shape[0], sl - cfg.x_len, D_MODEL), F32)], axis=1)
        return x.reshape(x.shape[0] * sl, D_MODEL)

    grouped = cfg.back_group > 1
    if grouped:
        assert not cfg.pipelined and not cfg.states_only and sl == SUBLANES and 2 * cfg.valid_len == sl
        mix_base = pl.multiple_of(lax.rem(step, cfg.back_group) * rows, rows)

    def mix_store(lo, hi, col_lo, col_hi, value):
        if grouped:
            mix_scr[pl.ds(mix_base + lo, hi - lo), col_lo:col_hi] = value
        else:
            mix_scr[lo:hi, col_lo:col_hi] = _bf(value)

    def proj(lo, hi):
        z_scr[:, lo:hi] = _dot(st["xn"], win_ref[:, lo:hi])

    if cfg.pipelined:
        @pl.when(step == 0)
        def _():
            mix_scr[...] = jnp.zeros(mix_scr.shape, mix_scr.dtype)

    tail_lo = (cfg.valid_len - 1) // SUBLANES * SUBLANES
    assert cfg.valid_len - tail_lo >= GDN_CONV_K - 1 and (cfg.n_tt == 1 or cfg.valid_len == sl)

    def causal_conv(x, prev_buf, w_ref, col, width):
        taps = w_ref.shape[0]
        t8 = lax.broadcasted_iota(jnp.int32, (nb, SUBLANES, width), 1)
        prev = prev_buf[:, :, col:col + width]
        acc = x * w_ref[taps - 1:taps, col:col + width]
        for back in range(1, taps):
            moved = pltpu.roll(x, back, 1)
            head = jnp.where(t8 < back, pltpu.roll(prev, back, 1), moved[:, :SUBLANES, :])
            moved = head if sl == SUBLANES else jnp.concatenate([head, moved[:, SUBLANES:, :]], axis=1)
            acc = acc + moved * w_ref[taps - 1 - back:taps - back, col:col + width]
        prev_buf[:, :, col:col + width] = x[:, tail_lo:tail_lo + SUBLANES, :]
        return acc

    def conv_q(col):
        width = GDN_HEADS * GDN_DK
        x = z_scr[:, O_QKV + col:O_QKV + col + width].reshape(nb, sl, width)
        act = _silu(causal_conv(x, xq_buf, cqw_ref, col, width).reshape(rows, width))
        if col == V_COL:
            qkv_scr[:, col:col + width] = act
            return
        scale = GDN_DK ** -0.5 if col == Q_COL else 1.0
        for h in range(GDN_HEADS):
            ah = act[:, GDN_DK * h:GDN_DK * (h + 1)]
            qkv_scr[:, col + GDN_DK * h:col + GDN_DK * (h + 1)] = (
                ah * (lax.rsqrt(jnp.sum(ah * ah, axis=-1, keepdims=True) + EPS) * scale))

    def conv_a():
        x = (z_scr[:, O_C:O_H] * z_scr[:, O_H:O_QKV]).reshape(nb, sl, CONV_CH)
        acc = causal_conv(x, xa_buf, caw_ref, 0, CONV_CH)
        mix_store(0, rows, 0, CONV_CH, z_scr[:, O_B:O_C] * acc.reshape(rows, CONV_CH))

    def gates():
        zab = z_scr[:, O_AB:IN_W_PAD]
        g = -jnp.exp(alog_ref[...]) * _softplus(zab + dt_ref[...])
        beta = jax.nn.sigmoid(zab)
        lane = lax.broadcasted_iota(jnp.int32, (rows, LANES), 1)
        gb = jnp.where(lane < GDN_HEADS, g, beta)
        if cfg.valid_len < sl:
            t_in_seq = lax.broadcasted_iota(jnp.int32, (rows, LANES), 0) & (sl - 1)
            gb = jnp.where(t_in_seq < cfg.valid_len, gb, 0.0)
        gb_scr[...] = gb
        ri = lax.broadcasted_iota(jnp.int32, (rows, rows), 0)
        rj = lax.broadcasted_iota(jnp.int32, (rows, rows), 1)
        lc = _log2(cfg.chunk)
        tri = jnp.where(((ri >> lc) == (rj >> lc)) & (rj <= ri), 1.0, 0.0).astype(BF16)
        g1 = _bf(gb)
        r1 = gb - g1.astype(F32)
        g2 = _bf(r1)
        g3 = _bf(r1 - g2.astype(F32))
        gc_scr[...] = _dot(tri, g1) + (_dot(tri, g2) + _dot(tri, g3))

    def gated_norm(lo, hi):
        for h in range(GDN_HEADS):
            oh = o_scr[lo:hi, GDN_DV * h:GDN_DV * (h + 1)]
            gate = z_scr[lo:hi, O_G + GDN_DV * h:O_G + GDN_DV * (h + 1)]
            mix_store(lo, hi, CONV_CH + GDN_DV * h, CONV_CH + GDN_DV * (h + 1),
                      _rms(oh, gnorm_ref[...]) * _silu(gate))

    def project(src_ref, groups):
        if "xn" not in st:
            st["xn"] = _bf(_rms(padded_rows(src_ref), nmix_ref[...]))
        for group in groups:
            if group == 0:
                proj(O_QKV, O_G)
            elif group == 1:
                proj(O_B, O_QKV)
            else:
                proj(O_G, O_AB)
                z_ab = _dot(st["xn"], win_ref[:, O_AB:O_AB + 2 * GDN_HEADS])
                z_scr[:, O_AB:] = jnp.pad(z_ab, ((0, 0), (0, LANES - 2 * GDN_HEADS)))

    def front_half():
        project(x_ref, (0, 1))
        yield
        conv_q(Q_COL)
        yield
        project(x_ref, (2,))
        yield
        conv_q(K_COL)
        yield
        conv_q(V_COL)
        yield
        conv_a()
        yield
        gates()
        yield
        for lo in range(0, len(problems), PREP_GROUP):
            for _ in _delta_prep(cfg, problems[lo:lo + PREP_GROUP], qkv_scr, gb_scr, gc_scr, st):
                yield
        per = cfg.chunk if cfg.carry else 2 * (VC_ROWS // GDN_HEADS)
        done = len(problems) * per
        if done < rows:
            o_scr[done:, :] = jnp.zeros((rows - done, GDN_HEADS * GDN_DV), F32)
            gated_norm(done, rows)
        yield
        for i in range(len(problems)):
            if cfg.carry:
                _delta_apply_carry(cfg, problems[i:i + 1], st["preps"][i:i + 1], s_scr, o_scr)
            else:
                _delta_apply_blocks(cfg, problems[i:i + 1], st["preps"][i:i + 1], ssm_ref, nssm_ref, o_scr)
            gated_norm(i * per, (i + 1) * per)
            yield

    def real_rows(v):
        g = v.reshape(v.shape[0] // (2 * sl), 2 * sl, v.shape[1])
        t8 = lax.broadcasted_iota(jnp.int32, (g.shape[0], sl, g.shape[2]), 1)
        return jnp.where(t8 < cfg.valid_len, g[:, :sl], pltpu.roll(g[:, sl:], cfg.valid_len, 1)).reshape(
            v.shape[0] // 2, v.shape[1])

    def padded_seqs(v):
        g = v.reshape(v.shape[0] // sl, sl, v.shape[1])
        return jnp.concatenate([g, pltpu.roll(g, cfg.valid_len, 1)], axis=1).reshape(
            v.shape[0] // cfg.valid_len, sl, v.shape[1])

    def back_half():
        if grouped:
            h1 = real_rows(padded_rows(xb_ref)) + _dot(_bf(real_rows(mix_scr[...])), wout_ref[...])
        else:
            h1 = padded_rows(xb_ref) + _dot(mix_scr[...], wout_ref[...])
        hn = _bf(_rms(h1, nffn_ref[...]))
        yield
        for c in range(FFN_HIDDEN // FFN_CHUNK):
            up = jnp.maximum(_dot(hn, wup_ref[:, c * FFN_CHUNK:(c + 1) * FFN_CHUNK]), 0.0)
            up_scr[:, c * FFN_CHUNK:(c + 1) * FFN_CHUNK] = _bf(up * up)
            yield
        outs = []
        for n in range(D_MODEL // FFN_CHUNK):
            cols = slice(n * FFN_CHUNK, (n + 1) * FFN_CHUNK)
            outs.append(h1[:, cols] + _dot(up_scr[...], wdown_ref[:, cols]))
            yield
        h2 = jnp.concatenate(outs, axis=1)
        y = _rms(h2, nfin_ref[...])
        y = padded_seqs(y) if grouped else y.reshape(nb, sl, D_MODEL)
        y_ref[...] = y[:, :cfg.x_len, :]

    back = back_half()
    if cfg.states_only:
        back = iter(())
        y_ref[...] = jnp.zeros(y_ref.shape, F32)
    if cfg.pipelined:
        next(back)
        for i, _ in enumerate(front_half()):
            for _ in range(cfg.back_after[i] if i < len(cfg.back_after) else 0):
                next(back, None)
    else:
        for _ in front_half():
            pass
    if grouped:
        @pl.when(lax.rem(step, cfg.back_group) == cfg.back_group - 1)
        def _():
            for _ in back:
                pass
    else:
        for _ in back:
            pass

    @pl.when(last)
    def _():
        end = cfg.valid_len - tail_lo
        nta_ref[...] = xa_buf[:, end - (SCONV_K - 1):end, :]
        base = 0 if ntq_ref.shape[1] == nb else (tile // cfg.n_tt) * nb
        for b in range(nb):
            ntq_ref[:, pl.ds(base + b, 1), :] = xq_buf[b, end - (GDN_CONV_K - 1):end, :].reshape(
                GDN_CONV_K - 1, 1, QKV_W)
        if cfg.carry:
            nssm_ref[0] = s_scr[...]


def _layer_call(cfg, name, x, tail_a, tail_q, ssm, params):
    nb, sl, rows = cfg.nb, cfg.seq_len, cfg.rows
    assert x.shape == (cfg.batch, cfg.x_len * cfg.n_tt, D_MODEL), (x.shape, cfg)
    assert cfg.batch % nb == 0 and sl % SUBLANES == 0 and rows % VC_ROWS == 0
    assert not cfg.shared_state or nb == 1
    n_tiles, n_tt = cfg.n_tiles, cfg.n_tt

    def front_tile(s):
        return jnp.minimum(s, n_tiles - 1) if cfg.pipelined else s

    def back_tile(s):
        return jnp.maximum(s - 1, 0) if cfg.pipelined else s

    def tile_spec(which):
        return pl.BlockSpec((nb, cfg.x_len, D_MODEL), lambda s: (which(s) // n_tt, which(s) % n_tt, 0))

    group = cfg.back_group
    assert group == 1 or (n_tt == 1 and n_tiles % group == 0)
    back_spec = tile_spec(back_tile) if group == 1 else pl.BlockSpec(
        (nb * group, cfg.x_len, D_MODEL), lambda s: (s // group, 0, 0))

    def state_spec(shape):
        zeros = (0,) * len(shape)
        if cfg.shared_state:
            return pl.BlockSpec((nb,) + shape, lambda s: (0,) + zeros)
        return pl.BlockSpec((nb,) + shape, lambda s: (front_tile(s) // n_tt,) + zeros)

    def out_state_spec(shape):
        zeros = (0,) * len(shape)
        return pl.BlockSpec((nb,) + shape, lambda s: (front_tile(s) // n_tt,) + zeros)

    def const_spec(arr):
        zeros = (0,) * arr.ndim
        return pl.BlockSpec(arr.shape, lambda s: zeros, pipeline_mode=pl.Buffered(1))

    ta_shape, ssm_shape = (SCONV_K - 1, CONV_CH), (GDN_HEADS, GDN_DK, GDN_DV)
    tq_block = (GDN_CONV_K - 1, nb, QKV_W)
    tq_in_spec = pl.BlockSpec(tq_block, lambda s: (0, 0 if cfg.shared_state else front_tile(s) // n_tt, 0))
    if nb % SUBLANES == 0 or nb == cfg.batch:
        tq_out_spec = pl.BlockSpec(tq_block, lambda s: (0, front_tile(s) // n_tt, 0))
    else:
        tq_out_spec = pl.BlockSpec((GDN_CONV_K - 1, cfg.batch, QKV_W), lambda s: (0, 0, 0))
    in_specs = [tile_spec(front_tile), back_spec, state_spec(ta_shape), tq_in_spec, state_spec(ssm_shape)]
    in_specs += [pl.BlockSpec(memory_space=pl.ANY) if cfg.states_only and i >= FIRST_BACK_PARAM else const_spec(p)
                 for i, p in enumerate(params)]
    out_specs = [back_spec, out_state_spec(ta_shape), tq_out_spec, out_state_spec(ssm_shape)]
    out_shape = [
        jax.ShapeDtypeStruct(x.shape, F32),
        jax.ShapeDtypeStruct((cfg.batch,) + ta_shape, F32),
        jax.ShapeDtypeStruct((GDN_CONV_K - 1, cfg.batch, QKV_W), F32),
        jax.ShapeDtypeStruct((cfg.batch,) + ssm_shape, F32),
    ]
    scratch = [
        pltpu.VMEM((rows, IN_W_PAD), F32),
        pltpu.VMEM((nb, SUBLANES, CONV_CH), F32),
        pltpu.VMEM((nb, SUBLANES, QKV_W), F32),
        pltpu.VMEM((rows, QKV_W), F32),
        pltpu.VMEM((rows, LANES), F32),
        pltpu.VMEM((rows, LANES), F32),
        pltpu.VMEM((rows, GDN_HEADS * GDN_DV), F32),
        pltpu.VMEM(ssm_shape, F32),
        pltpu.VMEM((rows * group, D_MODEL), BF16 if group == 1 else F32),
        pltpu.VMEM((rows if group == 1 else rows * group * cfg.valid_len // sl, FFN_HIDDEN), BF16),
    ]
    return pl.pallas_call(
        functools.partial(_layer_kernel, cfg),
        grid=(n_tiles + (1 if cfg.pipelined else 0),),
        in_specs=in_specs,
        out_specs=out_specs,
        out_shape=out_shape,
        scratch_shapes=scratch,
        compiler_params=pltpu.CompilerParams(
            dimension_semantics=("arbitrary",), vmem_limit_bytes=VMEM_LIMIT_BYTES),
        name=name,
    )(x, x, tail_a, tail_q, ssm, *params)


META_PAD_LEN = 128
FIRST_BACK_PARAM = 7
PROMPT_TILE = 256
SAMPLE_PAD_LEN = 8
SAMPLE_SEQS_PER_STEP = 16


def kernel(x_prompt, x_sample, state_conv_a, state_conv_qkv, state_ssm, meta_tokens, norm_mix, w_in, conv_a_w,
           conv_qkv_w, a_log, dt_bias, gdn_norm, w_out, norm_ffn, w_up, w_down, norm_final):
    assert norm_mix.shape[0] == 1, "single layer"
    bp, tp, _ = x_prompt.shape
    bs, ts, _ = x_sample.shape
    n_meta = meta_tokens.shape[0]

    def row(v):
        return v.reshape(1, -1).astype(F32)

    def taps_major(v):
        return jnp.transpose(v, (1, 0, 2))

    def lane_row(v):
        return jnp.pad(v.astype(F32), (0, LANES - v.shape[0])).reshape(1, LANES)

    params = (
        row(norm_mix[0]),
        _bf(w_in[0]),
        conv_a_w[0].astype(F32), conv_qkv_w[0].astype(F32),
        lane_row(a_log[0]), lane_row(dt_bias[0]), row(gdn_norm[0]),
        _bf(w_out[0]), row(norm_ffn[0]),
        _bf(w_up[0]),
        _bf(w_down[0]),
        row(norm_final),
    )

    meta_cfg = Cfg(batch=1, nb=1, seq_len=META_PAD_LEN, valid_len=n_meta, chunk=64, n_tt=1, shared_state=False,
                   pipelined=False, states_only=True)
    meta_x = meta_tokens.astype(F32)[None]
    _, ta_m, tq_m, ssm_m = _layer_call(
        meta_cfg, "layer_meta", meta_x,
        jnp.zeros((1, SCONV_K - 1, CONV_CH), F32), jnp.zeros((GDN_CONV_K - 1, 1, QKV_W), F32),
        jnp.zeros((1, GDN_HEADS, GDN_DK, GDN_DV), F32), params)

    prompt_cfg = Cfg(batch=bp, nb=1, seq_len=PROMPT_TILE, valid_len=PROMPT_TILE, chunk=64,
                     n_tt=tp // PROMPT_TILE, shared_state=True, pipelined=True, states_only=False,
                     back_after=PROMPT_BACK_AFTER)
    y_p, ta_p, tq_p, ssm_p = _layer_call(prompt_cfg, "layer_prompt", x_prompt, ta_m, tq_m, ssm_m, params)

    sample_cfg = Cfg(batch=bs, nb=SAMPLE_SEQS_PER_STEP, seq_len=SAMPLE_PAD_LEN, valid_len=ts, chunk=SAMPLE_PAD_LEN,
                     n_tt=1, shared_state=False, pipelined=False, states_only=False, back_group=2)
    y_s, ta_s, tq_s, ssm_s = _layer_call(
        sample_cfg, "layer_sample", x_sample, state_conv_a[0], taps_major(state_conv_qkv[0]), state_ssm[0], params)

    return (y_p, y_s, ta_p[None], taps_major(tq_p)[None], ssm_p[None], ta_s[None], taps_major(tq_s)[None], ssm_s[None])
```
